```python
import math
import jax
import jax.numpy as jnp
from jax import lax
import numpy as np

D_MODEL = 1024
BATCH = 1
SEQ = 16384
DEPTH = 4
DEC_BATCH = 8
DEC_SEQ = 64
PAST_LEN = 2048

CHUNK = 64
N_MIXERS = 2
N_GDN_LAYERS = (DEPTH + 1) // 2
N_SB_LAYERS = DEPTH // 2
MEM_TOKENS = 256
MEM_HEADS = 4
MEM_HEAD_DIM = 64
MEM_WIDTH = MEM_HEADS * MEM_HEAD_DIM
GDN_HEAD_DIM = 128
GDN_HEADS = 6
GDN_WIDTH = GDN_HEADS * GDN_HEAD_DIM
GDN_CONV = 4
GDN_CONV_CH = 3 * GDN_WIDTH
GDN_TOK_IN = 4 * GDN_WIDTH + 2 * GDN_HEADS
GDN_IN = GDN_TOK_IN + MEM_WIDTH
GDN_MIX = GDN_WIDTH + MEM_WIDTH
SB_HEAD_DIM = 128
SB_HEADS = 4
SB_WIDTH = SB_HEADS * SB_HEAD_DIM
SB_TOK_IN = 3 * SB_WIDTH
SB_IN = SB_TOK_IN + MEM_WIDTH
SB_MIX = SB_WIDTH + MEM_WIDTH
SB_BLOCK = 128
D_FF = 4 * D_MODEL
NORM_EPS = 1e-6

kernel_name = 'hybrid_gdn_stickbreak_stream_step'


def rms_norm(x, gain):
    xf = x.astype(jnp.float32)
    y = xf * lax.rsqrt(jnp.mean(xf * xf, axis=-1, keepdims=True) + NORM_EPS)
    return (y * gain.astype(jnp.float32)).astype(x.dtype)


def l2_norm(x):
    xf = x.astype(jnp.float32)
    return (xf * lax.rsqrt(jnp.sum(xf * xf, axis=-1, keepdims=True) + NORM_EPS)).astype(x.dtype)


def causal_conv_silu(x, buf, w):
    t = x.shape[1]
    xp = jnp.concatenate([buf.astype(x.dtype), x], axis=1)
    y = xp[:, 0:t] * w[0]
    for j in range(1, GDN_CONV):
        y = y + xp[:, j:j + t] * w[j]
    return jax.nn.silu(y), xp[:, -(GDN_CONV - 1):]


def gdn_chunked(q, k, v, beta, g, s0, chunk):
    f32 = jnp.float32
    b, t, h, dk = q.shape
    dv = v.shape[-1]
    n = t // chunk

    def blocks(a):
        a = a.astype(f32).reshape((b, n, chunk, h) + a.shape[3:])
        return jnp.swapaxes(jnp.moveaxis(a, 2, 3), 0, 1)

    qc, kc, vc = blocks(q), blocks(k), blocks(v)
    bc = blocks(beta)
    gc = jnp.cumsum(blocks(g), axis=-1)
    idx = jnp.arange(chunk)
    causal = idx[:, None] >= idx[None, :]
    strict = idx[:, None] > idx[None, :]
    diff = gc[..., :, None] - gc[..., None, :]
    decay = jnp.where(causal, jnp.exp(jnp.where(causal, diff, 0.0)), 0.0)
    kb = kc * bc[..., None]
    lower = jnp.einsum('nbhid,nbhjd->nbhij', kb, kc) * jnp.where(strict, decay, 0.0)
    lhs = lower + jnp.eye(chunk, dtype=f32)
    rhs = jnp.concatenate([vc * bc[..., None], kb * jnp.exp(gc)[..., None]], axis=-1)
    sol = lax.linalg.triangular_solve(lhs, rhs, left_side=True, lower=True, unit_diagonal=True)
    u, w = sol[..., :dv], sol[..., dv:]
    a_intra = jnp.einsum('nbhid,nbhjd->nbhij', qc, kc) * decay
    q_dec = qc * jnp.exp(gc)[..., None]
    k_dec = kc * jnp.exp(gc[..., -1:] - gc)[..., None]
    g_last = jnp.exp(gc[..., -1])[..., None, None]

    def step(s, inp):
        u_c, w_c, a_c, qd_c, kd_c, gl_c = inp
        v_new = u_c - jnp.einsum('bhck,bhkv->bhcv', w_c, s)
        o_c = jnp.einsum('bhck,bhkv->bhcv', qd_c, s) + jnp.einsum('bhij,bhjv->bhiv', a_c, v_new)
        s = s * gl_c + jnp.einsum('bhck,bhcv->bhkv', kd_c, v_new)
        return s, o_c

    s_fin, o = lax.scan(step, s0.astype(f32), (u, w, a_intra, q_dec, k_dec, g_last))
    o = jnp.moveaxis(jnp.swapaxes(o, 0, 1), 2, 3).reshape(b, t, h, dv)
    return o, s_fin


def gdn_mixer(tok, conv_buf, s0, conv_w, a_log, dt_bias, o_gain):
    b, t, _ = tok.shape
    tw = GDN_WIDTH
    qkv_raw = tok[..., :3 * tw]
    z = tok[..., 3 * tw:4 * tw]
    b_raw = tok[..., 4 * tw:4 * tw + GDN_HEADS]
    a_raw = tok[..., 4 * tw + GDN_HEADS:]
    qkv, new_buf = causal_conv_silu(qkv_raw, conv_buf, conv_w)
    q, k, v = (u.reshape(b, t, GDN_HEADS, GDN_HEAD_DIM) for u in jnp.split(qkv, 3, axis=-1))
    q = l2_norm(q) * (GDN_HEAD_DIM ** -0.5)
    k = l2_norm(k)
    beta = jax.nn.sigmoid(b_raw.astype(jnp.float32))
    g = -jnp.exp(a_log.astype(jnp.float32)) * jax.nn.softplus(a_raw.astype(jnp.float32) + dt_bias.astype(jnp.float32))
    o, s_new = gdn_chunked(q, k, v, beta, g, s0, min(CHUNK, t))
    o = rms_norm(o.astype(tok.dtype), o_gain) * jax.nn.silu(z.reshape(b, t, GDN_HEADS, GDN_HEAD_DIM))
    return o.reshape(b, t, tw), new_buf, s_new.astype(tok.dtype)


def sb_qkv(tok, q_gain, k_gain):
    b, t, _ = tok.shape
    q, k, v = (u.reshape(b, t, SB_HEADS, SB_HEAD_DIM) for u in jnp.split(tok, 3, axis=-1))
    return rms_norm(q, q_gain) * (SB_HEAD_DIM ** -0.5), rms_norm(k, k_gain), v


def sb_attend(q, k_dg, v_dg, k_off, v_off):
    f32 = jnp.float32
    qn = q.shape[1]
    idx = jnp.arange(qn)
    strict = idx[:, None] > idx[None, :]
    incl = (idx[:, None] >= idx[None, :]).astype(f32)
    z_d = jnp.einsum('bqhd,bshd->bhqs', q, k_dg).astype(f32)
    m_d = jnp.where(strict, jnp.log1p(jnp.exp(z_d)), 0.0)
    c_d = jnp.einsum('bhqj,js->bhqs', m_d, incl)
    a_d = jnp.where(strict, jnp.exp(z_d - c_d), 0.0)
    out = jnp.einsum('bhqs,bshd->bqhd', a_d.astype(v_dg.dtype), v_dg)
    if k_off is not None:
        nb = k_off.shape[1]
        bi = jnp.arange(SB_BLOCK)
        incl_b = (bi[:, None] >= bi[None, :]).astype(f32)
        ni = jnp.arange(nb)
        later = (ni[:, None] > ni[None, :]).astype(f32)
        z_o = jnp.einsum('bqhd,bnshd->bhqns', q, k_off).astype(f32)
        m_o = jnp.log1p(jnp.exp(z_o))
        c_o = jnp.einsum('bhqnj,js->bhqns', m_o, incl_b)
        suf = jnp.einsum('bhqm,mn->bhqn', c_o[..., 0], later) + c_d[..., 0:1]
        a_o = jnp.exp(z_o - c_o - suf[..., None])
        out = out + jnp.einsum('bhqns,bnshd->bqhd', a_o.astype(v_off.dtype), v_off)
    return out


def sb_prompt(q, k, v):
    b, t, h, d = q.shape
    nb = t // SB_BLOCK
    outs = []
    for i in range(nb):
        lo, hi = i * SB_BLOCK, (i + 1) * SB_BLOCK
        if i == 0:
            k_off, v_off = None, None
        else:
            k_off = k[:, :lo].reshape(b, i, SB_BLOCK, h, d)
            v_off = v[:, :lo].reshape(b, i, SB_BLOCK, h, d)
        outs.append(sb_attend(q[:, lo:hi], k[:, lo:hi], v[:, lo:hi], k_off, v_off))
    return jnp.concatenate(outs, axis=1).reshape(b, t, h * d)


def memory_kv(mem, norm_g, w_kv, k_gain):
    b, m, _ = mem.shape
    kv = rms_norm(mem, norm_g) @ w_kv
    k, v = jnp.split(kv, 2, axis=-1)
    k = rms_norm(k.reshape(b, m, MEM_HEADS, MEM_HEAD_DIM), k_gain)
    return k, v.reshape(b, m, MEM_HEADS, MEM_HEAD_DIM)


def memory_attend(mq, mk, mv, q_gain):
    b, t, _ = mq.shape
    q = rms_norm(mq.reshape(b, t, MEM_HEADS, MEM_HEAD_DIM), q_gain)
    s = jnp.einsum('bthd,bmhd->bhtm', q, mk.astype(q.dtype)).astype(jnp.float32) * (MEM_HEAD_DIM ** -0.5)
    p = jax.nn.softmax(s, axis=-1).astype(q.dtype)
    return jnp.einsum('bhtm,bmhd->bthd', p, mv.astype(q.dtype)).reshape(b, t, MEM_WIDTH)


def sq_relu_mlp(h, w_up, w_down):
    a = jax.nn.relu(h @ w_up)
    return (a * a) @ w_down


def setup_inputs(seed: int = 0) -> dict:
    key = jax.random.key(seed)
    ks = jax.random.split(key, 27)
    f32 = jnp.float32
    nrm = lambda k, shape, s=1.0: s * jax.random.normal(k, shape, f32)
    gain = lambda k, shape: 1.0 + 0.02 * jax.random.normal(k, shape, f32)
    dt = jnp.exp(jax.random.uniform(ks[19], (N_GDN_LAYERS, GDN_HEADS), f32) * (math.log(0.1) - math.log(0.001)) + math.log(0.001))
    return {
        'x_prompt': nrm(ks[0], (BATCH, SEQ, D_MODEL)),
        'x_sample': nrm(ks[1], (DEC_BATCH, DEC_SEQ, D_MODEL)),
        'state_gdn_conv': nrm(ks[2], (N_GDN_LAYERS, DEC_BATCH, GDN_CONV - 1, GDN_CONV_CH)),
        'state_gdn_s': nrm(ks[3], (N_GDN_LAYERS, DEC_BATCH, GDN_HEADS, GDN_HEAD_DIM, GDN_HEAD_DIM), 0.1),
        'cache_sb_k': nrm(ks[4], (N_SB_LAYERS, DEC_BATCH, PAST_LEN, SB_HEADS, SB_HEAD_DIM)),
        'cache_sb_v': nrm(ks[5], (N_SB_LAYERS, DEC_BATCH, PAST_LEN, SB_HEADS, SB_HEAD_DIM)),
        'cache_mem_k': nrm(ks[6], (DEPTH, DEC_BATCH, MEM_TOKENS, MEM_HEADS, MEM_HEAD_DIM)),
        'cache_mem_v': nrm(ks[7], (DEPTH, DEC_BATCH, MEM_TOKENS, MEM_HEADS, MEM_HEAD_DIM)),
        'mem_prompt': nrm(ks[8], (BATCH, MEM_TOKENS, D_MODEL)),
        'norm_mix': gain(ks[9], (DEPTH, D_MODEL)),
        'norm_mem': gain(ks[10], (DEPTH, D_MODEL)),
        'norm_ffn': gain(ks[11], (DEPTH, D_MODEL)),
        'w_in_gdn': nrm(ks[12], (N_GDN_LAYERS, D_MODEL, GDN_IN), D_MODEL ** -0.5),
        'w_in_sb': nrm(ks[13], (N_SB_LAYERS, D_MODEL, SB_IN), D_MODEL ** -0.5),
        'w_mem_kv': nrm(ks[14], (DEPTH, D_MODEL, 2 * MEM_WIDTH), D_MODEL ** -0.5),
        'mem_q_gain': gain(ks[15], (DEPTH, MEM_HEAD_DIM)),
        'mem_k_gain': gain(ks[16], (DEPTH, MEM_HEAD_DIM)),
        'gdn_conv_w': nrm(ks[17], (N_GDN_LAYERS, GDN_CONV, GDN_CONV_CH), GDN_CONV ** -0.5),
        'gdn_a_log': jnp.log(jax.random.uniform(ks[18], (N_GDN_LAYERS, GDN_HEADS), f32, 1.0, 16.0)),
        'gdn_dt_bias': dt + jnp.log(-jnp.expm1(-dt)),
        'gdn_o_gain': gain(ks[20], (N_GDN_LAYERS, GDN_HEAD_DIM)),
        'sb_q_gain': gain(ks[21], (N_SB_LAYERS, SB_HEAD_DIM)),
        'sb_k_gain': gain(ks[22], (N_SB_LAYERS, SB_HEAD_DIM)),
        'w_out_gdn': nrm(ks[23], (N_GDN_LAYERS, GDN_MIX, D_MODEL), GDN_MIX ** -0.5),
        'w_out_sb': nrm(ks[26], (N_SB_LAYERS, SB_MIX, D_MODEL), SB_MIX ** -0.5),
        'w_up': nrm(ks[24], (DEPTH, D_MODEL, D_FF), D_MODEL ** -0.5),
        'w_down': nrm(ks[25], (DEPTH, D_FF, D_MODEL), D_FF ** -0.5),
    }


def reference(x_prompt, x_sample, state_gdn_conv, state_gdn_s, cache_sb_k, cache_sb_v, cache_mem_k, cache_mem_v,
              mem_prompt, norm_mix, norm_mem, norm_ffn, w_in_gdn, w_in_sb, w_mem_kv, mem_q_gain, mem_k_gain,
              gdn_conv_w, gdn_a_log, gdn_dt_bias, gdn_o_gain, sb_q_gain, sb_k_gain, w_out_gdn, w_out_sb, w_up, w_down):
    yp, ys = x_prompt, x_sample
    bp = x_prompt.shape[0]
    bs, t_new = x_sample.shape[0], x_sample.shape[1]
    past = cache_sb_k.shape[2]
    n_past_blk = past // SB_BLOCK
    pc, pst, pk, pv, pmk, pmv = [], [], [], [], [], []
    sc, sst, sk, sv = [], [], [], []
    for i in range(DEPTH):
        j = i // N_MIXERS
        mk, mv = memory_kv(mem_prompt, norm_mem[i], w_mem_kv[i], mem_k_gain[i])
        pmk.append(mk)
        pmv.append(mv)
        hp = rms_norm(yp, norm_mix[i])
        hs = rms_norm(ys, norm_mix[i])
        if i % N_MIXERS == 0:
            pp = hp @ w_in_gdn[j]
            ps = hs @ w_in_gdn[j]
            tok_p, mq_p = pp[..., :GDN_TOK_IN], pp[..., GDN_TOK_IN:]
            tok_s, mq_s = ps[..., :GDN_TOK_IN], ps[..., GDN_TOK_IN:]
            buf0 = jnp.zeros((bp, GDN_CONV - 1, GDN_CONV_CH), tok_p.dtype)
            st0 = jnp.zeros((bp, GDN_HEADS, GDN_HEAD_DIM, GDN_HEAD_DIM), jnp.float32)
            op, bufp, stp = gdn_mixer(tok_p, buf0, st0, gdn_conv_w[j], gdn_a_log[j], gdn_dt_bias[j], gdn_o_gain[j])
            osm, bufs, sts = gdn_mixer(tok_s, state_gdn_conv[j], state_gdn_s[j], gdn_conv_w[j], gdn_a_log[j], gdn_dt_bias[j], gdn_o_gain[j])
            pc.append(bufp)
            pst.append(stp)
            sc.append(bufs)
            sst.append(sts)
            w_o = w_out_gdn[j]
        else:
            pp = hp @ w_in_sb[j]
            ps = hs @ w_in_sb[j]
            tok_p, mq_p = pp[..., :SB_TOK_IN], pp[..., SB_TOK_IN:]
            tok_s, mq_s = ps[..., :SB_TOK_IN], ps[..., SB_TOK_IN:]
            qp, kp, vp = sb_qkv(tok_p, sb_q_gain[j], sb_k_gain[j])
            op = sb_prompt(qp, kp, vp)
            qs, ks_, vs = sb_qkv(tok_s, sb_q_gain[j], sb_k_gain[j])
            k_past = cache_sb_k[j].astype(ks_.dtype).reshape(bs, n_past_blk, SB_BLOCK, SB_HEADS, SB_HEAD_DIM)
            v_past = cache_sb_v[j].astype(vs.dtype).reshape(bs, n_past_blk, SB_BLOCK, SB_HEADS, SB_HEAD_DIM)
            osm = sb_attend(qs, ks_, vs, k_past, v_past).reshape(bs, t_new, SB_WIDTH)
            pk.append(kp)
            pv.append(vp)
            sk.append(ks_)
            sv.append(vs)
            w_o = w_out_sb[j]
        mp = memory_attend(mq_p, mk, mv, mem_q_gain[i])
        ms = memory_attend(mq_s, cache_mem_k[i], cache_mem_v[i], mem_q_gain[i])
        yp = yp + jnp.concatenate([op, mp], axis=-1) @ w_o
        ys = ys + jnp.concatenate([osm, ms], axis=-1) @ w_o
        yp = yp + sq_relu_mlp(rms_norm(yp, norm_ffn[i]), w_up[i], w_down[i])
        ys = ys + sq_relu_mlp(rms_norm(ys, norm_ffn[i]), w_up[i], w_down[i])
    p_gdn_conv = jnp.stack(pc)
    p_gdn_s = jnp.stack(pst)
    p_sb_k = jnp.stack(pk)
    p_sb_v = jnp.stack(pv)
    p_mem_k = jnp.stack(pmk)
    p_mem_v = jnp.stack(pmv)
    s_gdn_conv = jnp.stack(sc)
    s_gdn_s = jnp.stack(sst)
    s_sb_k = jnp.stack(sk)
    s_sb_v = jnp.stack(sv)
    return (yp, ys, p_gdn_conv, p_gdn_s, p_sb_k, p_sb_v, p_mem_k, p_mem_v, s_gdn_conv, s_gdn_s, s_sb_k, s_sb_v)
```

```python
import functools
import math

import jax
import jax.numpy as jnp
from jax import lax
from jax.experimental import pallas as pl
from jax.experimental.pallas import tpu as pltpu

F32 = jnp.float32
BF16 = jnp.bfloat16
HIGHEST = lax.Precision.HIGHEST

NORM_EPS = 1e-6
CHUNK = 64
GDN_HEAD_DIM = 128
GDN_CONV = 4
SB_HEAD_DIM = 128
MEM_HEAD_DIM = 64
LANES = 128
SUBLANES = 8
VMEM_LIMIT_BYTES = 56 * 1024 * 1024
ROW_TILE = 512
COL_CHUNK = 512
F32_MIN_NORMAL_LOG = 88.0


def _params(*semantics):
    return pltpu.CompilerParams(dimension_semantics=semantics, vmem_limit_bytes=VMEM_LIMIT_BYTES)


def _dot(a, b):
    return jnp.dot(a.astype(BF16), b.astype(BF16), preferred_element_type=F32)


def _dot_nt(a, b):
    return lax.dot_general(a.astype(BF16), b.astype(BF16), (((1,), (1,)), ((), ())),
                           preferred_element_type=F32)


def _dot_f32(a, b, dims=(((1,), (0,)), ((), ()))):
    return lax.dot_general(a, b, dims, precision=HIGHEST, preferred_element_type=F32)


def _rms(x, gain):
    return x * lax.rsqrt(jnp.mean(x * x, axis=-1, keepdims=True) + NORM_EPS) * gain


def _sigmoid(x):
    return 1.0 / (1.0 + jnp.exp(-x))


def _softplus(x):
    return jnp.maximum(x, 0.0) + jnp.log1p(jnp.exp(-jnp.abs(x)))


def _const_spec(shape):
    zeros = (0,) * len(shape)
    return pl.BlockSpec(shape, lambda *_: zeros)


def _project(xb, w_ref, o_ref, off, n):
    for c in range(0, n, COL_CHUNK):
        cw = min(COL_CHUNK, n - c)
        o_ref[:, c:c + cw] = jnp.dot(xb, w_ref[:, off + c:off + c + cw],
                                     preferred_element_type=F32).astype(o_ref.dtype)


def _in_proj_gdn_kernel(x_ref, g_ref, w_ref, qkv_ref, z_ref, ba_ref, mq_ref):
    xb = _rms(x_ref[...], g_ref[...]).astype(BF16)
    off = 0
    for o_ref in (qkv_ref, z_ref, ba_ref, mq_ref):
        n = o_ref.shape[-1]
        _project(xb, w_ref, o_ref, off, n)
        off += n


def _in_proj_sb_kernel(x_ref, g_ref, w_ref, gq_ref, gk_ref, q_ref, k_ref, v_ref, kb_ref, vb_ref, mq_ref):
    xb = _rms(x_ref[...], g_ref[...]).astype(BF16)
    width = k_ref.shape[-1]
    for h in range(width // SB_HEAD_DIM):
        cs = slice(h * SB_HEAD_DIM, (h + 1) * SB_HEAD_DIM)
        q = jnp.dot(xb, w_ref[:, h * SB_HEAD_DIM:(h + 1) * SB_HEAD_DIM], preferred_element_type=F32)
        q_ref[:, cs] = (_rms(q, gq_ref[...]) * (SB_HEAD_DIM ** -0.5)).astype(q_ref.dtype)
        k = jnp.dot(xb, w_ref[:, width + h * SB_HEAD_DIM:width + (h + 1) * SB_HEAD_DIM],
                    preferred_element_type=F32)
        k = _rms(k, gk_ref[...])
        k_ref[:, cs] = k
        kb_ref[:, cs] = k.astype(BF16)
        v = jnp.dot(xb, w_ref[:, 2 * width + h * SB_HEAD_DIM:2 * width + (h + 1) * SB_HEAD_DIM],
                    preferred_element_type=F32)
        v_ref[:, cs] = v
        vb_ref[:, cs] = v.astype(BF16)
    _project(xb, w_ref, mq_ref, 3 * width, mq_ref.shape[-1])


def _row_spec(tm, n):
    return pl.BlockSpec((tm, n), lambda i: (i, 0))


def _in_proj_gdn(x2, gain, w, widths):
    rows, d = x2.shape
    tm = min(ROW_TILE, rows)
    return pl.pallas_call(
        _in_proj_gdn_kernel,
        out_shape=[jax.ShapeDtypeStruct((rows, n), F32) for n in widths],
        grid=(rows // tm,),
        in_specs=[_row_spec(tm, d), _const_spec((1, d)), _const_spec(w.shape)],
        out_specs=[_row_spec(tm, n) for n in widths],
        compiler_params=_params("parallel"),
        name="in_proj_gdn",
    )(x2, gain, w)


def _in_proj_sb(x2, gain, w, gq, gk, width, mem_width):
    rows, d = x2.shape
    tm = min(ROW_TILE, rows)
    shapes = [(width, BF16), (width, F32), (width, F32), (width, BF16), (width, BF16), (mem_width, F32)]
    return pl.pallas_call(
        _in_proj_sb_kernel,
        out_shape=[jax.ShapeDtypeStruct((rows, n), dt) for n, dt in shapes],
        grid=(rows // tm,),
        in_specs=[_row_spec(tm, d), _const_spec((1, d)), _const_spec(w.shape),
                  _const_spec((1, SB_HEAD_DIM)), _const_spec((1, SB_HEAD_DIM))],
        out_specs=[_row_spec(tm, n) for n, _ in shapes],
        compiler_params=_params("parallel"),
        name="in_proj_sb",
    )(x2, gain, w, gq, gk)


def _mem_kv_kernel(mem_ref, g_ref, w_ref, kg_ref, hm_ref, k_ref, v_ref):
    xb = _rms(mem_ref[...], g_ref[...]).astype(BF16)
    width = k_ref.shape[-1]
    k = jnp.dot(xb, w_ref[:, :width], preferred_element_type=F32)
    ms = _dot_f32(k * k, hm_ref[...])
    k_ref[...] = k * lax.rsqrt(ms + NORM_EPS) * kg_ref[...]
    v_ref[...] = jnp.dot(xb, w_ref[:, width:], preferred_element_type=F32)


def _mem_kv(mem, norm_mem, w_kv, k_gain_t, head_mean):
    bp, m, d = mem.shape
    depth = w_kv.shape[0]
    width = w_kv.shape[-1] // 2
    return pl.pallas_call(
        _mem_kv_kernel,
        out_shape=[jax.ShapeDtypeStruct((depth, bp, m, width), F32)] * 2,
        grid=(depth, bp),
        in_specs=[pl.BlockSpec((None, m, d), lambda i, b: (b, 0, 0)),
                  pl.BlockSpec((None, 1, d), lambda i, b: (i, 0, 0)),
                  pl.BlockSpec((None, d, 2 * width), lambda i, b: (i, 0, 0)),
                  pl.BlockSpec((None, 1, width), lambda i, b: (i, 0, 0)),
                  _const_spec((width, width))],
        out_specs=[pl.BlockSpec((None, None, m, width), lambda i, b: (i, b, 0, 0))] * 2,
        compiler_params=_params("parallel", "parallel"),
        name="mem_kv",
    )(mem, norm_mem, w_kv, k_gain_t, head_mean)


def _mix_out_kernel(o_ref, mq_ref, kbd_ref, vbd_ref, qg_ref, hm_ref, w_ref, x_ref, y_ref, *, n_mem):
    q = mq_ref[...]
    ms = _dot_f32(q * q, hm_ref[...])
    qn = q * lax.rsqrt(ms + NORM_EPS) * qg_ref[...]
    s = _dot(qn, kbd_ref[...]) * (MEM_HEAD_DIM ** -0.5)
    heads = s.shape[-1] // n_mem
    mem = None
    for h in range(heads):
        sh = s[:, h * n_mem:(h + 1) * n_mem]
        e = jnp.exp(sh - jnp.max(sh, axis=-1, keepdims=True))
        p = e / jnp.sum(e, axis=-1, keepdims=True)
        part = _dot(p, vbd_ref[h * n_mem:(h + 1) * n_mem, :])
        mem = part if mem is None else mem + part
    w_mix = o_ref.shape[-1]
    y = x_ref[...] + _dot(o_ref[...], w_ref[:w_mix, :]) + _dot(mem, w_ref[w_mix:, :])
    y_ref[...] = y


def _mix_out(o_mix, mq, kbd, vbd, q_gain_t, head_mean, w_out, x):
    b, t, d = x.shape
    w_mix = o_mix.shape[-1]
    mw = mq.shape[-1]
    n_mem = vbd.shape[1] // (mw // MEM_HEAD_DIM)
    tm = min(ROW_TILE, t)
    tok = lambda n: pl.BlockSpec((None, tm, n), lambda bi, i: (bi, i, 0))
    return pl.pallas_call(
        functools.partial(_mix_out_kernel, n_mem=n_mem),
        out_shape=jax.ShapeDtypeStruct((b, t, d), F32),
        grid=(b, t // tm),
        in_specs=[tok(w_mix), tok(mw),
                  pl.BlockSpec((None,) + kbd.shape[1:], lambda bi, i: (bi, 0, 0)),
                  pl.BlockSpec((None,) + vbd.shape[1:], lambda bi, i: (bi, 0, 0)),
                  _const_spec((1, mw)), _const_spec((mw, mw)), _const_spec(w_out.shape), tok(d)],
        out_specs=tok(d),
        compiler_params=_params("parallel", "parallel"),
        name="mix_out",
    )(o_mix, mq, kbd, vbd, q_gain_t, head_mean, w_out, x)


FF_CHUNK = 1024


def _ffn_kernel(x_ref, g_ref, wu_ref, wd_ref, y_ref):
    x = x_ref[...]
    hb = _rms(x, g_ref[...]).astype(BF16)
    acc = x
    for c in range(0, wu_ref.shape[-1], FF_CHUNK):
        a = jnp.maximum(jnp.dot(hb, wu_ref[:, c:c + FF_CHUNK], preferred_element_type=F32), 0.0)
        acc = acc + jnp.dot((a * a).astype(BF16), wd_ref[c:c + FF_CHUNK, :], preferred_element_type=F32)
    y_ref[...] = acc


def _ffn(x2, gain, w_up, w_down):
    rows, d = x2.shape
    tm = min(ROW_TILE, rows)
    return pl.pallas_call(
        _ffn_kernel,
        out_shape=jax.ShapeDtypeStruct((rows, d), F32),
        grid=(rows // tm,),
        in_specs=[_row_spec(tm, d), _const_spec((1, d)), _const_spec(w_up.shape), _const_spec(w_down.shape)],
        out_specs=_row_spec(tm, d),
        compiler_params=_params("parallel"),
        name="ffn",
    )(x2, gain, w_up, w_down)


CONV_PAD = SUBLANES


def _gdn_kernel(qkv_ref, z_ref, ba_ref, buf_ref, s0_ref, cw_ref, nal_ref, dt_ref, og_ref,
                o_ref, nbuf_ref, sout_ref, xp_ref, act_ref, s_ref, *, heads):
    blk = pl.program_id(1)
    tb = qkv_ref.shape[0]
    hd = GDN_HEAD_DIM
    width = heads * hd
    tail = GDN_CONV - 1

    @pl.when(blk == 0)
    def _():
        xp_ref[CONV_PAD - tail:CONV_PAD, :] = buf_ref[...]
        s_ref[...] = s0_ref[...]

    xp_ref[CONV_PAD:CONV_PAD + tb, :] = qkv_ref[...]

    for ct in range(3 * heads):
        cs = slice(ct * hd, (ct + 1) * hd)
        y = None
        for j in range(GDN_CONV):
            r0 = CONV_PAD - tail + j
            term = xp_ref[r0:r0 + tb, cs] * cw_ref[j:j + 1, cs]
            y = term if y is None else y + term
        y = y * _sigmoid(y)
        if ct < 2 * heads:
            y = y * lax.rsqrt(jnp.sum(y * y, axis=-1, keepdims=True) + NORM_EPS)
            if ct < heads:
                y = y * (hd ** -0.5)
        act_ref[:, cs] = y

    new_tail = xp_ref[CONV_PAD + tb - tail:CONV_PAD + tb, :]
    xp_ref[CONV_PAD - tail:CONV_PAD, :] = new_tail

    ba = ba_ref[...]
    beta_all = _sigmoid(ba)
    g_all = nal_ref[...] * _softplus(ba + dt_ref[...])

    row = lax.broadcasted_iota(jnp.int32, (CHUNK, CHUNK), 0)
    col = lax.broadcasted_iota(jnp.int32, (CHUNK, CHUNK), 1)
    causal = row >= col
    strict = row > col
    tri_f = causal.astype(F32)
    upper_f = (row <= col).astype(F32)
    eye = (row == col).astype(F32)

    for c in range(tb // CHUNK):
        rs = slice(c * CHUNK, (c + 1) * CHUNK)
        g_c = g_all[rs, :]
        gc_all = _dot_f32(tri_f, g_c)
        for h in range(heads):
            gl = heads + h
            gcol = gc_all[:, gl:gl + 1]
            grow = jnp.sum(g_c[:, gl:gl + 1] * upper_f, axis=0, keepdims=True)
            glast = gc_all[CHUNK - 1:CHUNK, gl:gl + 1]
            decay = jnp.where(causal, jnp.exp(jnp.where(causal, gcol - grow, 0.0)), 0.0)
            bcol = beta_all[rs, h:h + 1]
            q = act_ref[rs, h * hd:(h + 1) * hd]
            k = act_ref[rs, width + h * hd:width + (h + 1) * hd]
            v = act_ref[rs, 2 * width + h * hd:2 * width + (h + 1) * hd]
            kb = k * bcol
            egc = jnp.exp(gcol)
            lower = _dot_f32(kb, k, (((1,), (1,)), ((), ()))) * jnp.where(strict, decay, 0.0)
            p = -lower
            inv = eye + p
            span = 2
            while span < CHUNK:
                p = _dot_f32(p, p)
                inv = inv + _dot_f32(inv, p)
                span *= 2
            rhs = jnp.concatenate([v * bcol, kb * egc], axis=-1)
            sol = _dot_f32(inv, rhs)
            u = sol[:, :hd]
            w = sol[:, hd:]
            a_intra = _dot_f32(q, k, (((1,), (1,)), ((), ()))) * decay
            q_dec = q * egc
            k_dec = k * jnp.exp(glast - gcol)
            s = s_ref[h]
            v_new = u - _dot_f32(w, s)
            o = _dot_f32(q_dec, s) + _dot_f32(a_intra, v_new)
            s_ref[h] = s * jnp.exp(glast) + _dot_f32(k_dec, v_new, (((0,), (0,)), ((), ())))
            zh = z_ref[rs, h * hd:(h + 1) * hd]
            o_ref[rs, h * hd:(h + 1) * hd] = _rms(o, og_ref[...]) * (zh * _sigmoid(zh))

    @pl.when(blk == pl.num_programs(1) - 1)
    def _():
        nbuf_ref[...] = new_tail
        sout_ref[...] = s_ref[...]


def _gdn_mixer(qkv, z, ba, conv_buf, s0, conv_w, neg_exp_alog, dt_bias, o_gain, tb):
    b, t, cw = qkv.shape
    heads = s0.shape[1]
    hd = GDN_HEAD_DIM
    width = heads * hd
    tail = GDN_CONV - 1
    tok = lambda n: pl.BlockSpec((None, tb, n), lambda bi, i: (bi, i, 0))
    return pl.pallas_call(
        functools.partial(_gdn_kernel, heads=heads),
        out_shape=[jax.ShapeDtypeStruct((b, t, width), F32),
                   jax.ShapeDtypeStruct((b, tail, cw), F32),
                   jax.ShapeDtypeStruct((b, heads, hd, hd), F32)],
        grid=(b, t // tb),
        in_specs=[tok(cw), tok(width), tok(LANES),
                  pl.BlockSpec((None, tail, cw), lambda bi, i: (bi, 0, 0)),
                  pl.BlockSpec((None, heads, hd, hd), lambda bi, i: (bi, 0, 0, 0)),
                  _const_spec((GDN_CONV, cw)), _const_spec((1, LANES)), _const_spec((1, LANES)),
                  _const_spec((1, hd))],
        out_specs=[tok(width),
                   pl.BlockSpec((None, tail, cw), lambda bi, i: (bi, 0, 0)),
                   pl.BlockSpec((None, heads, hd, hd), lambda bi, i: (bi, 0, 0, 0))],
        scratch_shapes=[pltpu.VMEM((CONV_PAD + tb, cw), F32), pltpu.VMEM((tb, cw), F32),
                        pltpu.VMEM((heads, hd, hd), F32)],
        compiler_params=_params("parallel", "arbitrary"),
        name="gdn_mixer",
    )(qkv, z, ba, conv_buf, s0, conv_w, neg_exp_alog, dt_bias, o_gain)


def _suffix_sums(m, tri_b):
    m_hi = m.astype(BF16)
    m_lo = (m - m_hi.astype(F32)).astype(BF16)
    return (jnp.dot(m_hi, tri_b, preferred_element_type=F32)
            + jnp.dot(m_lo, tri_b, preferred_element_type=F32))


def _sb_attn_kernel(q_ref, kd_ref, vd_ref, ko_ref, vo_ref, gq_ref, gk_ref, o_ref, acc_ref, suf_ref,
                    *, tk, off_blocks):
    tq = q_ref.shape[0]
    q = q_ref[...].astype(BF16)

    row = lax.broadcasted_iota(jnp.int32, (tq, tq), 0)
    col = lax.broadcasted_iota(jnp.int32, (tq, tq), 1)
    strict = row > col
    z = _dot_nt(q, kd_ref[...])
    m = jnp.where(strict, jnp.log1p(jnp.exp(z)), 0.0)
    c = _suffix_sums(m, (row >= col).astype(BF16))
    a = jnp.where(strict, jnp.exp(z - c), 0.0)
    acc_ref[...] = _dot(a, vd_ref[...])
    suf_ref[...] = c[:, 0:1]

    krow = lax.broadcasted_iota(jnp.int32, (tk, tk), 0)
    kcol = lax.broadcasted_iota(jnp.int32, (tk, tk), 1)
    tri_k = (krow >= kcol).astype(BF16)
    z_bound = (jnp.max(jnp.abs(gq_ref[...])) * jnp.max(jnp.abs(gk_ref[...]))
               * (math.sqrt(SB_HEAD_DIM) * 1.01))
    n_off = off_blocks(pl.program_id(2))

    def live(carry):
        n, suf_min = carry
        return jnp.logical_and(n >= 0, z_bound - suf_min > -F32_MIN_NORMAL_LOG)

    def step(carry):
        n, _ = carry
        start = pl.multiple_of(n * tk, tk)
        kb = ko_ref[pl.ds(start, tk), :]
        vb = vo_ref[pl.ds(start, tk), :]
        zo = _dot_nt(q, kb)
        mo = jnp.log1p(jnp.exp(zo))
        co = _suffix_sums(mo, tri_k)
        suf = suf_ref[...]
        ao = jnp.exp(zo - co - suf)
        acc_ref[...] += _dot(ao, vb)
        suf = suf + co[:, 0:1]
        suf_ref[...] = suf
        return n - 1, jnp.min(suf)

    lax.while_loop(live, step, (n_off - 1, jnp.min(c[:, 0:1])))
    o_ref[...] = acc_ref[...]


def _sb_attention(q, k_new, v_new, k_off, v_off, gq, gk, tq, tk, causal_prompt):
    b, t, width = q.shape
    d = SB_HEAD_DIM
    heads = width // d
    t_off = k_off.shape[1]
    if causal_prompt:
        off_blocks = lambda i: (i * tq) // tk
    else:
        off_blocks = lambda i: t_off // tk
    blk = pl.BlockSpec((None, tq, d), lambda bi, h, i: (bi, i, h))
    full = pl.BlockSpec((None, t_off, d), lambda bi, h, i: (bi, 0, h))
    return pl.pallas_call(
        functools.partial(_sb_attn_kernel, tk=tk, off_blocks=off_blocks),
        out_shape=jax.ShapeDtypeStruct((b, t, width), F32),
        grid=(b, heads, t // tq),
        in_specs=[blk, blk, blk, full, full, _const_spec((1, d)), _const_spec((1, d))],
        out_specs=blk,
        scratch_shapes=[pltpu.VMEM((tq, d), F32), pltpu.VMEM((tq, 1), F32)],
        compiler_params=_params("parallel", "parallel", "arbitrary"),
        name="sb_attention",
    )(q, k_new, v_new, k_off, v_off, gq, gk)


def _block_diag_kv(mk, mv):
    b, m, h, dh = mk.shape
    eye = jnp.eye(h, dtype=mk.dtype)
    kbd = jnp.einsum("bmhd,hg->bhdgm", mk, eye).reshape(b, h * dh, h * m)
    vbd = jnp.einsum("bmhd,hg->bhmgd", mv, eye).reshape(b, h * m, h * dh)
    return kbd.astype(BF16), vbd.astype(BF16)


def _head_mean_matrix(width, head_dim):
    idx = jnp.arange(width) // head_dim
    return (idx[:, None] == idx[None, :]).astype(F32) / head_dim


def kernel(x_prompt, x_sample, state_gdn_conv, state_gdn_s, cache_sb_k, cache_sb_v, cache_mem_k, cache_mem_v,
           mem_prompt, norm_mix, norm_mem, norm_ffn, w_in_gdn, w_in_sb, w_mem_kv, mem_q_gain, mem_k_gain,
           gdn_conv_w, gdn_a_log, gdn_dt_bias, gdn_o_gain, sb_q_gain, sb_k_gain, w_out_gdn, w_out_sb, w_up, w_down):
    depth, d_model = norm_mix.shape
    bp, t_p, _ = x_prompt.shape
    bs, t_s, _ = x_sample.shape
    gdn_heads = gdn_a_log.shape[1]
    gdn_width = gdn_heads * GDN_HEAD_DIM
    conv_ch = gdn_conv_w.shape[-1]
    sb_width = cache_sb_k.shape[3] * cache_sb_k.shape[4]
    mem_tokens, mem_heads = cache_mem_k.shape[2], cache_mem_k.shape[3]
    mem_width = mem_heads * MEM_HEAD_DIM

    n_gate = 2 * gdn_heads
    tok_in = conv_ch + gdn_width + n_gate
    w_g = w_in_gdn
    w_gdn = jnp.concatenate(
        [w_g[..., :conv_ch + gdn_width],
         jnp.pad(w_g[..., conv_ch + gdn_width:tok_in], ((0, 0), (0, 0), (0, LANES - n_gate))),
         w_g[..., tok_in:]], axis=-1).astype(BF16)
    gdn_widths = (conv_ch, gdn_width, LANES, mem_width)
    w_sb = w_in_sb.astype(BF16)
    w_kv = w_mem_kv.astype(BF16)
    w_og = w_out_gdn.astype(BF16)
    w_os = w_out_sb.astype(BF16)
    w_u = w_up.astype(BF16)
    w_d = w_down.astype(BF16)

    head_mean = _head_mean_matrix(mem_width, MEM_HEAD_DIM)
    q_gain_t = jnp.tile(mem_q_gain, (1, mem_heads))[:, None, :]
    k_gain_t = jnp.tile(mem_k_gain, (1, mem_heads))[:, None, :]
    lane_pad = lambda a: jnp.pad(a, ((0, 0), (gdn_heads, LANES - n_gate)))[:, None, :]
    neg_exp_alog = lane_pad(-jnp.exp(gdn_a_log))
    dt_bias = lane_pad(gdn_dt_bias)

    pmk, pmv = _mem_kv(mem_prompt, norm_mem[:, None, :], w_kv, k_gain_t, head_mean)
    pmk5 = pmk.reshape(depth, bp, mem_tokens, mem_heads, MEM_HEAD_DIM)
    pmv5 = pmv.reshape(depth, bp, mem_tokens, mem_heads, MEM_HEAD_DIM)

    yp, ys = x_prompt, x_sample
    pc, pst, pk, pv = [], [], [], []
    sc, sst, sk, sv = [], [], [], []
    for i in range(depth):
        j = i // 2
        g_mix = norm_mix[i][None, :]
        paths = []
        for y, cmk, cmv, is_prompt in ((yp, pmk5[i], pmv5[i], True), (ys, cache_mem_k[i], cache_mem_v[i], False)):
            b, t, _ = y.shape
            y2 = y.reshape(b * t, d_model)
            kbd, vbd = _block_diag_kv(cmk, cmv)
            if i % 2 == 0:
                qkv, z, ba, mq = _in_proj_gdn(y2, g_mix, w_gdn[j], gdn_widths)
                if is_prompt:
                    buf0 = jnp.zeros((b, GDN_CONV - 1, conv_ch), F32)
                    st0 = jnp.zeros((b, gdn_heads, GDN_HEAD_DIM, GDN_HEAD_DIM), F32)
                else:
                    buf0, st0 = state_gdn_conv[j], state_gdn_s[j]
                o_mix, nbuf, s_new = _gdn_mixer(
                    qkv.reshape(b, t, conv_ch), z.reshape(b, t, gdn_width), ba.reshape(b, t, LANES),
                    buf0, st0, gdn_conv_w[j], neg_exp_alog[j], dt_bias[j], gdn_o_gain[j][None, :],
                    tb=min(t, CHUNK))
                (pc if is_prompt else sc).append(nbuf)
                (pst if is_prompt else sst).append(s_new)
                w_o = w_og[j]
            else:
                gq, gk = sb_q_gain[j][None, :], sb_k_gain[j][None, :]
                q, k, v, kb, vb, mq = _in_proj_sb(y2, g_mix, w_sb[j], gq, gk, sb_width, mem_width)
                r3 = lambda a: a.reshape(b, t, sb_width)
                if is_prompt:
                    tq = min(t, 256)
                    o_mix = _sb_attention(r3(q), r3(kb), r3(vb), r3(kb), r3(vb), gq, gk, tq, tq, True)
                else:
                    past = cache_sb_k.shape[2]
                    o_mix = _sb_attention(r3(q), r3(kb), r3(vb), cache_sb_k[j].reshape(b, past, sb_width),
                                          cache_sb_v[j].reshape(b, past, sb_width), gq, gk, t, min(past, 128), False)
                heads4 = lambda a: a.reshape(b, t, sb_width // SB_HEAD_DIM, SB_HEAD_DIM)
                (pk if is_prompt else sk).append(heads4(k))
                (pv if is_prompt else sv).append(heads4(v))
                w_o = w_os[j]
            y = _mix_out(o_mix, mq.reshape(b, t, mem_width), kbd, vbd, q_gain_t[i], head_mean, w_o, y)
            y = _ffn(y.reshape(b * t, d_model), norm_ffn[i][None, :], w_u[i], w_d[i]).reshape(b, t, d_model)
            paths.append(y)
        yp, ys = paths
    return (yp, ys, jnp.stack(pc), jnp.stack(pst), jnp.stack(pk), jnp.stack(pv), pmk5, pmv5,
            jnp.stack(sc), jnp.stack(sst), jnp.stack(sk), jnp.stack(sv))
```

```python
import functools
import math

import jax
import jax.numpy as jnp
from jax import lax
from jax.experimental import pallas as pl
from jax.experimental.pallas import tpu as pltpu

F32 = jnp.float32
BF16 = jnp.bfloat16

NORM_EPS = 1e-6
CHUNK = 64
GDN_HEAD_DIM = 128
GDN_CONV = 4
GDN_BLOCK = 256
SB_HEAD_DIM = 128
SB_QUERY_BLOCK = 256
SB_CACHE_BLOCK = 128
MEM_HEAD_DIM = 64
LANES = 128
SUBLANES = 8
VMEM_LIMIT_BYTES = 56 * 1024 * 1024
ROW_TILE = 512
COL_CHUNK = 512
FF_CHUNK = 1024
F32_MIN_NORMAL_LOG = 88.0


def _params(*semantics):
    return pltpu.CompilerParams(dimension_semantics=semantics, vmem_limit_bytes=VMEM_LIMIT_BYTES)


def _dot(a, b):
    return jnp.dot(a.astype(BF16), b.astype(BF16), preferred_element_type=F32)


def _dot_nt(a, b):
    return lax.dot_general(a.astype(BF16), b.astype(BF16), (((1,), (1,)), ((), ())),
                           preferred_element_type=F32)


def _dot_tn(a, b):
    return lax.dot_general(a.astype(BF16), b.astype(BF16), (((0,), (0,)), ((), ())),
                           preferred_element_type=F32)


def _split_bf16(x, terms):
    parts = []
    for _ in range(terms):
        p = x.astype(BF16)
        parts.append(p)
        x = x - p.astype(F32)
    return parts


def _dot_split(a_parts, b_parts):
    (a_hi, a_lo), (b_hi, b_lo) = a_parts, b_parts
    return (jnp.dot(a_hi, b_hi, preferred_element_type=F32) + jnp.dot(a_hi, b_lo, preferred_element_type=F32)
            + jnp.dot(a_lo, b_hi, preferred_element_type=F32))


def _dot_exact_rhs(a, b_exact, terms):
    acc = None
    for p in _split_bf16(a, terms):
        t = jnp.dot(p, b_exact, preferred_element_type=F32)
        acc = t if acc is None else acc + t
    return acc


def _dot_exact_lhs(a_exact, b, terms):
    acc = None
    for p in _split_bf16(b, terms):
        t = jnp.dot(a_exact, p, preferred_element_type=F32)
        acc = t if acc is None else acc + t
    return acc


def _rms(x, gain):
    return x * lax.rsqrt(jnp.mean(x * x, axis=-1, keepdims=True) + NORM_EPS) * gain


def _sigmoid(x):
    return 1.0 / (1.0 + jnp.exp(-x))


def _softplus(x):
    return jnp.maximum(x, 0.0) + jnp.log1p(jnp.exp(-jnp.abs(x)))


def _const_spec(shape):
    zeros = (0,) * len(shape)
    return pl.BlockSpec(shape, lambda *_: zeros)


def _row_spec(tm, n):
    return pl.BlockSpec((tm, n), lambda i: (i, 0))


def _project(xb, w_ref, o_ref, off, n):
    for c in range(0, n, COL_CHUNK):
        cw = min(COL_CHUNK, n - c)
        o_ref[:, c:c + cw] = jnp.dot(xb, w_ref[:, off + c:off + c + cw],
                                     preferred_element_type=F32).astype(o_ref.dtype)


def _in_proj_gdn_kernel(x_ref, g_ref, w_ref, qkv_ref, z_ref, ba_ref, mq_ref):
    xb = _rms(x_ref[...], g_ref[...]).astype(BF16)
    off = 0
    for o_ref in (qkv_ref, z_ref, ba_ref, mq_ref):
        n = o_ref.shape[-1]
        _project(xb, w_ref, o_ref, off, n)
        off += n


def _in_proj_gdn(x2, gain, w, widths):
    rows, d = x2.shape
    tm = min(ROW_TILE, rows)
    return pl.pallas_call(
        _in_proj_gdn_kernel,
        out_shape=[jax.ShapeDtypeStruct((rows, n), F32) for n in widths],
        grid=(rows // tm,),
        in_specs=[_row_spec(tm, d), _const_spec((1, d)), _const_spec(w.shape)],
        out_specs=[_row_spec(tm, n) for n in widths],
        compiler_params=_params("parallel"),
        name="in_proj_gdn",
    )(x2, gain, w)


def _in_proj_sb_kernel(x_ref, g_ref, w_ref, gq_ref, gk_ref, *refs, heads):
    q_ref, kb_ref, vb_ref, mq_ref, k_ref, v_ref = refs[-6:]
    xb = _rms(x_ref[...], g_ref[...]).astype(BF16)
    tm = x_ref.shape[0]
    d = SB_HEAD_DIM
    width = heads * d
    for h in range(heads):
        cs = slice(h * d, (h + 1) * d)
        interleaved = pl.ds(h, tm, stride=heads)
        q = jnp.dot(xb, w_ref[:, h * d:(h + 1) * d], preferred_element_type=F32)
        q_ref[:, cs] = (_rms(q, gq_ref[...]) * (d ** -0.5)).astype(BF16)
        k = jnp.dot(xb, w_ref[:, width + h * d:width + (h + 1) * d], preferred_element_type=F32)
        k = _rms(k, gk_ref[...])
        k_ref[interleaved, :] = k
        kb_ref[:, cs] = k.astype(BF16)
        v = jnp.dot(xb, w_ref[:, 2 * width + h * d:2 * width + (h + 1) * d], preferred_element_type=F32)
        v_ref[interleaved, :] = v
        vb_ref[:, cs] = v.astype(BF16)
    _project(xb, w_ref, mq_ref, 3 * width, mq_ref.shape[-1])


def _in_proj_sb(x2, gain, w, gq, gk, heads, mem_width, slot, n_slots, kv_all):
    rows, d_model = x2.shape
    d = SB_HEAD_DIM
    width = heads * d
    tm = min(ROW_TILE, rows)
    stack_shape = jax.ShapeDtypeStruct((n_slots, rows * heads, d), F32)
    stack_spec = pl.BlockSpec((None, tm * heads, d), lambda i: (slot, i, 0))
    in_specs = [_row_spec(tm, d_model), _const_spec((1, d_model)), _const_spec(w.shape),
                _const_spec((1, d)), _const_spec((1, d))]
    args = [x2, gain, w, gq, gk]
    aliases = {}
    if kv_all is not None:
        in_specs += [pl.BlockSpec(memory_space=pl.ANY)] * 2
        args += list(kv_all)
        aliases = {5: 4, 6: 5}
    return pl.pallas_call(
        functools.partial(_in_proj_sb_kernel, heads=heads),
        out_shape=[jax.ShapeDtypeStruct((rows, width), BF16)] * 3
        + [jax.ShapeDtypeStruct((rows, mem_width), F32), stack_shape, stack_shape],
        grid=(rows // tm,),
        in_specs=in_specs,
        out_specs=[_row_spec(tm, width)] * 3 + [_row_spec(tm, mem_width), stack_spec, stack_spec],
        input_output_aliases=aliases,
        compiler_params=_params("parallel"),
        name="in_proj_sb",
    )(*args)


def _head_mean_sq(x, hm_ref):
    return _dot_exact_rhs(x * x, hm_ref[...], 3)


def _mem_kv_kernel(mem_ref, g_ref, w_ref, kg_ref, hm_ref, k_ref, v_ref):
    xb = _rms(mem_ref[...], g_ref[...]).astype(BF16)
    width = k_ref.shape[-1]
    k = jnp.dot(xb, w_ref[:, :width], preferred_element_type=F32)
    k_ref[...] = k * lax.rsqrt(_head_mean_sq(k, hm_ref) + NORM_EPS) * kg_ref[...]
    v_ref[...] = jnp.dot(xb, w_ref[:, width:], preferred_element_type=F32)


def _mem_kv(mem, norm_mem, w_kv, k_gain_t, head_mean):
    bp, m, d = mem.shape
    depth = w_kv.shape[0]
    width = w_kv.shape[-1] // 2
    return pl.pallas_call(
        _mem_kv_kernel,
        out_shape=[jax.ShapeDtypeStruct((depth, bp, m, width), F32)] * 2,
        grid=(depth, bp),
        in_specs=[pl.BlockSpec((None, m, d), lambda i, b: (b, 0, 0)),
                  pl.BlockSpec((None, 1, d), lambda i, b: (i, 0, 0)),
                  pl.BlockSpec((None, d, 2 * width), lambda i, b: (i, 0, 0)),
                  pl.BlockSpec((None, 1, width), lambda i, b: (i, 0, 0)),
                  _const_spec((width, width))],
        out_specs=[pl.BlockSpec((None, None, m, width), lambda i, b: (i, b, 0, 0))] * 2,
        compiler_params=_params("parallel", "parallel"),
        name="mem_kv",
    )(mem, norm_mem, w_kv, k_gain_t, head_mean)


def _mix_out_kernel(o_ref, mq_ref, kbd_ref, vbd_ref, qg_ref, hm_ref, w_ref, x_ref, y_ref, *, n_mem):
    q = mq_ref[...]
    qn = q * lax.rsqrt(_head_mean_sq(q, hm_ref) + NORM_EPS) * qg_ref[...]
    s = _dot(qn, kbd_ref[...]) * (MEM_HEAD_DIM ** -0.5)
    heads = s.shape[-1] // n_mem
    mem = None
    for h in range(heads):
        sh = s[:, h * n_mem:(h + 1) * n_mem]
        e = jnp.exp(sh - jnp.max(sh, axis=-1, keepdims=True))
        p = e / jnp.sum(e, axis=-1, keepdims=True)
        part = _dot(p, vbd_ref[h * n_mem:(h + 1) * n_mem, :])
        mem = part if mem is None else mem + part
    w_mix = o_ref.shape[-1]
    y = x_ref[...] + _dot(o_ref[...], w_ref[:w_mix, :]) + _dot(mem, w_ref[w_mix:, :])
    y_ref[...] = y


def _mix_out(o_mix, mq, kbd, vbd, q_gain_t, head_mean, w_out, x):
    b, t, d = x.shape
    w_mix = o_mix.shape[-1]
    mw = mq.shape[-1]
    n_mem = vbd.shape[1] // (mw // MEM_HEAD_DIM)
    tm = min(ROW_TILE, t)
    tok = lambda n: pl.BlockSpec((None, tm, n), lambda bi, i: (bi, i, 0))
    return pl.pallas_call(
        functools.partial(_mix_out_kernel, n_mem=n_mem),
        out_shape=jax.ShapeDtypeStruct((b, t, d), F32),
        grid=(b, t // tm),
        in_specs=[tok(w_mix), tok(mw),
                  pl.BlockSpec((None,) + kbd.shape[1:], lambda bi, i: (bi, 0, 0)),
                  pl.BlockSpec((None,) + vbd.shape[1:], lambda bi, i: (bi, 0, 0)),
                  _const_spec((1, mw)), _const_spec((mw, mw)), _const_spec(w_out.shape), tok(d)],
        out_specs=tok(d),
        compiler_params=_params("parallel", "parallel"),
        name="mix_out",
    )(o_mix, mq, kbd, vbd, q_gain_t, head_mean, w_out, x)


def _ffn_kernel(x_ref, g_ref, wu_ref, wd_ref, y_ref):
    x = x_ref[...]
    hb = _rms(x, g_ref[...]).astype(BF16)
    acc = x
    for c in range(0, wu_ref.shape[-1], FF_CHUNK):
        a = jnp.maximum(jnp.dot(hb, wu_ref[:, c:c + FF_CHUNK], preferred_element_type=F32), 0.0)
        acc = acc + jnp.dot((a * a).astype(BF16), wd_ref[c:c + FF_CHUNK, :], preferred_element_type=F32)
    y_ref[...] = acc


def _ffn(x2, gain, w_up, w_down):
    rows, d = x2.shape
    tm = min(ROW_TILE, rows)
    return pl.pallas_call(
        _ffn_kernel,
        out_shape=jax.ShapeDtypeStruct((rows, d), F32),
        grid=(rows // tm,),
        in_specs=[_row_spec(tm, d), _const_spec((1, d)), _const_spec(w_up.shape), _const_spec(w_down.shape)],
        out_specs=_row_spec(tm, d),
        compiler_params=_params("parallel"),
        name="ffn",
    )(x2, gain, w_up, w_down)


CONV_PAD = SUBLANES


def _gdn_kernel(qkv_ref, z_ref, ba_ref, buf_ref, s0_ref, cw_ref, nal_ref, dt_ref, og_ref,
                o_ref, nbuf_ref, sout_ref,
                xp_ref, act_ref, s_ref, gc_ref, kw_ref, ku_ref, qw_ref, au_ref, *, heads):
    blk = pl.program_id(1)
    tb = qkv_ref.shape[0]
    hd = GDN_HEAD_DIM
    width = heads * hd
    tail = GDN_CONV - 1
    n_chunks = tb // CHUNK

    @pl.when(blk == 0)
    def _():
        xp_ref[CONV_PAD - tail:CONV_PAD, :] = buf_ref[...]
        s_ref[...] = s0_ref[...]

    xp_ref[CONV_PAD:CONV_PAD + tb, :] = qkv_ref[...]

    for ct in range(3 * heads):
        cs = slice(ct * hd, (ct + 1) * hd)
        y = None
        for j in range(GDN_CONV):
            r0 = CONV_PAD - tail + j
            term = xp_ref[r0:r0 + tb, cs] * cw_ref[j:j + 1, cs]
            y = term if y is None else y + term
        y = y * _sigmoid(y)
        if ct < 2 * heads:
            y = y * lax.rsqrt(jnp.sum(y * y, axis=-1, keepdims=True) + NORM_EPS)
            if ct < heads:
                y = y * (hd ** -0.5)
        act_ref[:, cs] = y

    new_tail = xp_ref[CONV_PAD + tb - tail:CONV_PAD + tb, :]
    xp_ref[CONV_PAD - tail:CONV_PAD, :] = new_tail

    ba = ba_ref[...]
    beta_all = _sigmoid(ba)
    g_all = nal_ref[...] * _softplus(ba + dt_ref[...])

    row = lax.broadcasted_iota(jnp.int32, (CHUNK, CHUNK), 0)
    col = lax.broadcasted_iota(jnp.int32, (CHUNK, CHUNK), 1)
    causal = row >= col
    strict = row > col
    tri_b = causal.astype(BF16)
    upper_f = (row <= col).astype(F32)
    eye = (row == col).astype(F32)

    for c in range(n_chunks):
        rs = slice(c * CHUNK, (c + 1) * CHUNK)
        g_c = g_all[rs, :]
        gc_all = _dot_exact_lhs(tri_b, g_c, 3)
        gc_ref[rs, :] = gc_all
        for h in range(heads):
            idx = c * heads + h
            gl = heads + h
            gcol = gc_all[:, gl:gl + 1]
            grow = jnp.sum(g_c[:, gl:gl + 1] * upper_f, axis=0, keepdims=True)
            glast = gc_all[CHUNK - 1:CHUNK, gl:gl + 1]
            decay = jnp.where(causal, jnp.exp(jnp.where(causal, gcol - grow, 0.0)), 0.0)
            bcol = beta_all[rs, h:h + 1]
            q = act_ref[rs, h * hd:(h + 1) * hd]
            k = act_ref[rs, width + h * hd:width + (h + 1) * hd]
            v = act_ref[rs, 2 * width + h * hd:2 * width + (h + 1) * hd]
            kb = k * bcol
            egc = jnp.exp(gcol)
            kq = _dot_nt(jnp.concatenate([kb, q], axis=0), k)
            lower = kq[:CHUNK] * jnp.where(strict, decay, 0.0)
            a_intra = kq[CHUNK:] * decay
            p = -lower
            inv = eye + p
            p_parts = _split_bf16(p, 2)
            span = 2
            while span < CHUNK:
                p_parts = _split_bf16(_dot_split(p_parts, p_parts), 2)
                inv = inv + _dot_split(_split_bf16(inv, 2), p_parts)
                span *= 2
            sol = _dot(inv, jnp.concatenate([v * bcol, kb * egc], axis=-1)).astype(BF16)
            k_dec = k * jnp.exp(glast - gcol)
            kd_sol = _dot_tn(k_dec, sol)
            a_sol = _dot(a_intra, sol)
            ku_ref[idx] = kd_sol[:, :hd]
            kw_ref[idx] = kd_sol[:, hd:].astype(BF16)
            au_ref[idx] = a_sol[:, :hd]
            qw_ref[idx] = (q * egc - a_sol[:, hd:]).astype(BF16)

    for c in range(n_chunks):
        rs = slice(c * CHUNK, (c + 1) * CHUNK)
        for h in range(heads):
            idx = c * heads + h
            gl = heads + h
            s = s_ref[h]
            sb = s.astype(BF16)
            glast = gc_ref[c * CHUNK + CHUNK - 1:(c + 1) * CHUNK, gl:gl + 1]
            o = jnp.dot(qw_ref[idx], sb, preferred_element_type=F32) + au_ref[idx]
            s_ref[h] = (s * jnp.exp(glast) - jnp.dot(kw_ref[idx], sb, preferred_element_type=F32)
                        + ku_ref[idx])
            zh = z_ref[rs, h * hd:(h + 1) * hd]
            o_ref[rs, h * hd:(h + 1) * hd] = _rms(o, og_ref[...]) * (zh * _sigmoid(zh))

    @pl.when(blk == pl.num_programs(1) - 1)
    def _():
        nbuf_ref[...] = new_tail
        sout_ref[...] = s_ref[...]


def _gdn_mixer(qkv, z, ba, conv_buf, s0, conv_w, neg_exp_alog, dt_bias, o_gain):
    b, t, cw = qkv.shape
    heads = s0.shape[1]
    hd = GDN_HEAD_DIM
    width = heads * hd
    tail = GDN_CONV - 1
    tb = min(t, GDN_BLOCK)
    n_ch = (tb // CHUNK) * heads
    tok = lambda n: pl.BlockSpec((None, tb, n), lambda bi, i: (bi, i, 0))
    return pl.pallas_call(
        functools.partial(_gdn_kernel, heads=heads),
        out_shape=[jax.ShapeDtypeStruct((b, t, width), F32),
                   jax.ShapeDtypeStruct((b, tail, cw), F32),
                   jax.ShapeDtypeStruct((b, heads, hd, hd), F32)],
        grid=(b, t // tb),
        in_specs=[tok(cw), tok(width), tok(LANES),
                  pl.BlockSpec((None, tail, cw), lambda bi, i: (bi, 0, 0)),
                  pl.BlockSpec((None, heads, hd, hd), lambda bi, i: (bi, 0, 0, 0)),
                  _const_spec((GDN_CONV, cw)), _const_spec((1, LANES)), _const_spec((1, LANES)),
                  _const_spec((1, hd))],
        out_specs=[tok(width),
                   pl.BlockSpec((None, tail, cw), lambda bi, i: (bi, 0, 0)),
                   pl.BlockSpec((None, heads, hd, hd), lambda bi, i: (bi, 0, 0, 0))],
        scratch_shapes=[pltpu.VMEM((CONV_PAD + tb, cw), F32), pltpu.VMEM((tb, cw), F32),
                        pltpu.VMEM((heads, hd, hd), F32), pltpu.VMEM((tb, LANES), F32),
                        pltpu.VMEM((n_ch, hd, hd), BF16), pltpu.VMEM((n_ch, hd, hd), F32),
                        pltpu.VMEM((n_ch, CHUNK, hd), BF16), pltpu.VMEM((n_ch, CHUNK, hd), F32)],
        compiler_params=_params("parallel", "arbitrary"),
        name="gdn_mixer",
    )(qkv, z, ba, conv_buf, s0, conv_w, neg_exp_alog, dt_bias, o_gain)


def _suffix_sums(m, tri_b):
    return _dot_exact_rhs(m, tri_b, 2)


def _sb_attn_kernel(q_ref, kd_ref, vd_ref, ko_ref, vo_ref, gq_ref, gk_ref, o_ref, acc_ref, suf_ref,
                    *, heads, tk, off_blocks, interleaved):
    tq = q_ref.shape[0]
    d = SB_HEAD_DIM

    row = lax.broadcasted_iota(jnp.int32, (tq, tq), 0)
    col = lax.broadcasted_iota(jnp.int32, (tq, tq), 1)
    strict = row > col
    tri_q = (row >= col).astype(BF16)
    suf_min = None
    for h in range(heads):
        cs = slice(h * d, (h + 1) * d)
        z = _dot_nt(q_ref[:, cs], kd_ref[:, cs])
        m = jnp.where(strict, jnp.log1p(jnp.exp(z)), 0.0)
        c = _suffix_sums(m, tri_q)
        a = jnp.where(strict, jnp.exp(z - c), 0.0)
        acc_ref[:, cs] = _dot(a, vd_ref[:, cs])
        suf_ref[h] = c[:, 0:1]
        hmin = jnp.min(c[:, 0:1])
        suf_min = hmin if suf_min is None else jnp.minimum(suf_min, hmin)

    krow = lax.broadcasted_iota(jnp.int32, (tk, tk), 0)
    kcol = lax.broadcasted_iota(jnp.int32, (tk, tk), 1)
    tri_k = (krow >= kcol).astype(BF16)
    z_bound = (jnp.max(jnp.abs(gq_ref[...])) * jnp.max(jnp.abs(gk_ref[...]))
               * (math.sqrt(SB_HEAD_DIM) * 1.01))
    n_off = off_blocks(pl.program_id(1))

    def live(carry):
        n, smin = carry
        return jnp.logical_and(n >= 0, z_bound - smin > -F32_MIN_NORMAL_LOG)

    def step(carry):
        n, _ = carry
        smin = None
        for h in range(heads):
            cs = slice(h * d, (h + 1) * d)
            if interleaved:
                rows = pl.ds(n * (tk * heads) + h, tk, stride=heads)
                kb, vb = ko_ref[rows, :], vo_ref[rows, :]
            else:
                rows = pl.ds(pl.multiple_of(n * tk, tk), tk)
                kb, vb = ko_ref[rows, cs], vo_ref[rows, cs]
            zo = _dot_nt(q_ref[:, cs], kb)
            co = _suffix_sums(jnp.log1p(jnp.exp(zo)), tri_k)
            suf = suf_ref[h]
            acc_ref[:, cs] += _dot(jnp.exp(zo - co - suf), vb)
            suf = suf + co[:, 0:1]
            suf_ref[h] = suf
            hmin = jnp.min(suf)
            smin = hmin if smin is None else jnp.minimum(smin, hmin)
        return n - 1, smin

    lax.while_loop(live, step, (n_off - 1, suf_min))
    o_ref[...] = acc_ref[...]


def _sb_attention(q, k_new, v_new, k_off, v_off, gq, gk, tq, tk, causal_prompt):
    b, t, width = q.shape
    d = SB_HEAD_DIM
    heads = width // d
    if causal_prompt:
        off_blocks = lambda i: (i * tq) // tk
    else:
        t_off = k_off.shape[1] // heads
        off_blocks = lambda i: t_off // tk
    blk = pl.BlockSpec((None, tq, width), lambda bi, i: (bi, i, 0))
    full = pl.BlockSpec((None,) + k_off.shape[1:], lambda bi, i: (bi, 0, 0), pipeline_mode=pl.Buffered(1))
    return pl.pallas_call(
        functools.partial(_sb_attn_kernel, heads=heads, tk=tk, off_blocks=off_blocks,
                          interleaved=not causal_prompt),
        out_shape=jax.ShapeDtypeStruct((b, t, width), F32),
        grid=(b, t // tq),
        in_specs=[blk, blk, blk, full, full, _const_spec((1, d)), _const_spec((1, d))],
        out_specs=blk,
        scratch_shapes=[pltpu.VMEM((tq, width), F32), pltpu.VMEM((heads, tq, 1), F32)],
        compiler_params=_params("parallel", "arbitrary"),
        name="sb_attention",
    )(q, k_new, v_new, k_off, v_off, gq, gk)


def _block_diag_kv(mk, mv):
    b, m, h, dh = mk.shape
    eye = jnp.eye(h, dtype=mk.dtype)
    kbd = jnp.einsum("bmhd,hg->bhdgm", mk, eye).reshape(b, h * dh, h * m)
    vbd = jnp.einsum("bmhd,hg->bhmgd", mv, eye).reshape(b, h * m, h * dh)
    return kbd.astype(BF16), vbd.astype(BF16)


def _head_mean_matrix(width, head_dim):
    idx = jnp.arange(width) // head_dim
    return ((idx[:, None] == idx[None, :]).astype(F32) / head_dim).astype(BF16)


def kernel(x_prompt, x_sample, state_gdn_conv, state_gdn_s, cache_sb_k, cache_sb_v, cache_mem_k, cache_mem_v,
           mem_prompt, norm_mix, norm_mem, norm_ffn, w_in_gdn, w_in_sb, w_mem_kv, mem_q_gain, mem_k_gain,
           gdn_conv_w, gdn_a_log, gdn_dt_bias, gdn_o_gain, sb_q_gain, sb_k_gain, w_out_gdn, w_out_sb, w_up, w_down):
    depth, d_model = norm_mix.shape
    bp, t_p, _ = x_prompt.shape
    bs, t_s, _ = x_sample.shape
    gdn_heads = gdn_a_log.shape[1]
    gdn_width = gdn_heads * GDN_HEAD_DIM
    conv_ch = gdn_conv_w.shape[-1]
    n_sb, _, past, sb_heads, _ = cache_sb_k.shape
    sb_width = sb_heads * SB_HEAD_DIM
    mem_tokens, mem_heads = cache_mem_k.shape[2], cache_mem_k.shape[3]
    mem_width = mem_heads * MEM_HEAD_DIM
    assert MEM_HEAD_DIM & (MEM_HEAD_DIM - 1) == 0

    n_gate = 2 * gdn_heads
    tok_in = conv_ch + gdn_width + n_gate
    w_g = w_in_gdn
    w_gdn = jnp.concatenate(
        [w_g[..., :conv_ch + gdn_width],
         jnp.pad(w_g[..., conv_ch + gdn_width:tok_in], ((0, 0), (0, 0), (0, LANES - n_gate))),
         w_g[..., tok_in:]], axis=-1).astype(BF16)
    gdn_widths = (conv_ch, gdn_width, LANES, mem_width)
    w_sb = w_in_sb.astype(BF16)
    w_kv = w_mem_kv.astype(BF16)
    w_og = w_out_gdn.astype(BF16)
    w_os = w_out_sb.astype(BF16)
    w_u = w_up.astype(BF16)
    w_d = w_down.astype(BF16)

    head_mean = _head_mean_matrix(mem_width, MEM_HEAD_DIM)
    q_gain_t = jnp.tile(mem_q_gain, (1, mem_heads))[:, None, :]
    k_gain_t = jnp.tile(mem_k_gain, (1, mem_heads))[:, None, :]
    lane_pad = lambda a: jnp.pad(a, ((0, 0), (gdn_heads, LANES - n_gate)))[:, None, :]
    neg_exp_alog = lane_pad(-jnp.exp(gdn_a_log))
    dt_bias = lane_pad(gdn_dt_bias)

    pmk, pmv = _mem_kv(mem_prompt, norm_mem[:, None, :], w_kv, k_gain_t, head_mean)
    pmk5 = pmk.reshape(depth, bp, mem_tokens, mem_heads, MEM_HEAD_DIM)
    pmv5 = pmv.reshape(depth, bp, mem_tokens, mem_heads, MEM_HEAD_DIM)

    yp, ys = x_prompt, x_sample
    pc, pst, sc, sst = [], [], [], []
    kv_stacks = {True: None, False: None}
    for i in range(depth):
        j = i // 2
        g_mix = norm_mix[i][None, :]
        paths = []
        for y, cmk, cmv, is_prompt in ((yp, pmk5[i], pmv5[i], True), (ys, cache_mem_k[i], cache_mem_v[i], False)):
            b, t, _ = y.shape
            y2 = y.reshape(b * t, d_model)
            kbd, vbd = _block_diag_kv(cmk, cmv)
            if i % 2 == 0:
                qkv, z, ba, mq = _in_proj_gdn(y2, g_mix, w_gdn[j], gdn_widths)
                if is_prompt:
                    buf0 = jnp.zeros((b, GDN_CONV - 1, conv_ch), F32)
                    st0 = jnp.zeros((b, gdn_heads, GDN_HEAD_DIM, GDN_HEAD_DIM), F32)
                else:
                    buf0, st0 = state_gdn_conv[j], state_gdn_s[j]
                o_mix, nbuf, s_new = _gdn_mixer(
                    qkv.reshape(b, t, conv_ch), z.reshape(b, t, gdn_width), ba.reshape(b, t, LANES),
                    buf0, st0, gdn_conv_w[j], neg_exp_alog[j], dt_bias[j], gdn_o_gain[j][None, :])
                (pc if is_prompt else sc).append(nbuf)
                (pst if is_prompt else sst).append(s_new)
                w_o = w_og[j]
            else:
                gq, gk = sb_q_gain[j][None, :], sb_k_gain[j][None, :]
                q, kb, vb, mq, k_all, v_all = _in_proj_sb(y2, g_mix, w_sb[j], gq, gk, sb_heads, mem_width,
                                                          j, n_sb, kv_stacks[is_prompt])
                kv_stacks[is_prompt] = (k_all, v_all)
                r3 = lambda a: a.reshape(b, t, sb_width)
                if is_prompt:
                    tq = min(t, SB_QUERY_BLOCK)
                    o_mix = _sb_attention(r3(q), r3(kb), r3(vb), r3(kb), r3(vb), gq, gk, tq, tq, True)
                else:
                    o_mix = _sb_attention(r3(q), r3(kb), r3(vb),
                                          cache_sb_k[j].reshape(b, past * sb_heads, SB_HEAD_DIM),
                                          cache_sb_v[j].reshape(b, past * sb_heads, SB_HEAD_DIM),
                                          gq, gk, t, min(past, SB_CACHE_BLOCK), False)
                w_o = w_os[j]
            y = _mix_out(o_mix, mq.reshape(b, t, mem_width), kbd, vbd, q_gain_t[i], head_mean, w_o, y)
            y = _ffn(y.reshape(b * t, d_model), norm_ffn[i][None, :], w_u[i], w_d[i]).reshape(b, t, d_model)
            paths.append(y)
        yp, ys = paths
    heads5 = lambda a, b, t: a.reshape(n_sb, b, t, sb_heads, SB_HEAD_DIM)
    pk, pv = (heads5(a, bp, t_p) for a in kv_stacks[True])
    sk, sv = (heads5(a, bs, t_s) for a in kv_stacks[False])
    return (yp, ys, jnp.stack(pc), jnp.stack(pst), pk, pv, pmk5, pmv5,
            jnp.stack(sc), jnp.stack(sst), sk, sv)
```

```python
import functools
import math

import jax
import jax.numpy as jnp
from jax import lax
from jax.experimental import pallas as pl
from jax.experimental.pallas import tpu as pltpu

F32 = jnp.float32
BF16 = jnp.bfloat16

NORM_EPS = 1e-6
CHUNK = 64
GDN_HEAD_DIM = 128
GDN_CONV = 4
GDN_BLOCK = 256
GDN_GROUP = 128
SB_HEAD_DIM = 128
SB_QUERY_BLOCK = 256
SB_CACHE_BLOCK = 128
MEM_HEAD_DIM = 64
LANES = 128
SUBLANES = 8
VMEM_LIMIT_BYTES = 56 * 1024 * 1024
ROW_TILE = 512
COL_CHUNK = 512
FF_CHUNK = 1024
F32_MIN_NORMAL_LOG = 88.0


def _params(*semantics):
    return pltpu.CompilerParams(dimension_semantics=semantics, vmem_limit_bytes=VMEM_LIMIT_BYTES)


def _dot(a, b):
    return jnp.dot(a.astype(BF16), b.astype(BF16), preferred_element_type=F32)


def _dot_nt(a, b):
    return lax.dot_general(a.astype(BF16), b.astype(BF16), (((1,), (1,)), ((), ())),
                           preferred_element_type=F32)


def _dot_tn(a, b):
    return lax.dot_general(a.astype(BF16), b.astype(BF16), (((0,), (0,)), ((), ())),
                           preferred_element_type=F32)


def _split_bf16(x, terms):
    parts = []
    for _ in range(terms):
        p = x.astype(BF16)
        parts.append(p)
        x = x - p.astype(F32)
    return parts


def _dot_exact_rhs(a, b_exact, terms):
    acc = None
    for p in _split_bf16(a, terms):
        t = jnp.dot(p, b_exact, preferred_element_type=F32)
        acc = t if acc is None else acc + t
    return acc


def _rms(x, gain):
    return x * lax.rsqrt(jnp.mean(x * x, axis=-1, keepdims=True) + NORM_EPS) * gain


def _sigmoid(x):
    return 1.0 / (1.0 + jnp.exp(-x))


def _softplus(x):
    return jnp.maximum(x, 0.0) + jnp.log1p(jnp.exp(-jnp.abs(x)))


def _const_spec(shape):
    zeros = (0,) * len(shape)
    return pl.BlockSpec(shape, lambda *_: zeros)


def _row_spec(tm, n):
    return pl.BlockSpec((tm, n), lambda i: (i, 0))


def _project(xb, w_ref, o_ref, off, n):
    for c in range(0, n, COL_CHUNK):
        cw = min(COL_CHUNK, n - c)
        o_ref[:, c:c + cw] = jnp.dot(xb, w_ref[:, off + c:off + c + cw],
                                     preferred_element_type=F32).astype(o_ref.dtype)


def _in_proj_gdn_kernel(x_ref, g_ref, w_ref, qkv_ref, z_ref, ba_ref, mq_ref):
    xb = _rms(x_ref[...], g_ref[...]).astype(BF16)
    off = 0
    for o_ref in (qkv_ref, z_ref, ba_ref, mq_ref):
        n = o_ref.shape[-1]
        _project(xb, w_ref, o_ref, off, n)
        off += n


def _in_proj_gdn(x2, gain, w, widths):
    rows, d = x2.shape
    tm = min(ROW_TILE, rows)
    return pl.pallas_call(
        _in_proj_gdn_kernel,
        out_shape=[jax.ShapeDtypeStruct((rows, n), F32) for n in widths],
        grid=(rows // tm,),
        in_specs=[_row_spec(tm, d), _const_spec((1, d)), _const_spec(w.shape)],
        out_specs=[_row_spec(tm, n) for n in widths],
        compiler_params=_params("parallel"),
        name="in_proj_gdn",
    )(x2, gain, w)


def _in_proj_sb_kernel(x_ref, g_ref, w_ref, gq_ref, gk_ref, *refs, heads):
    q_ref, kb_ref, vb_ref, mq_ref, k_ref, v_ref = refs[-6:]
    xb = _rms(x_ref[...], g_ref[...]).astype(BF16)
    tm = x_ref.shape[0]
    d = SB_HEAD_DIM
    width = heads * d
    for h in range(heads):
        cs = slice(h * d, (h + 1) * d)
        interleaved = pl.ds(h, tm, stride=heads)
        q = jnp.dot(xb, w_ref[:, h * d:(h + 1) * d], preferred_element_type=F32)
        q_ref[:, cs] = (_rms(q, gq_ref[...]) * (d ** -0.5)).astype(BF16)
        k = jnp.dot(xb, w_ref[:, width + h * d:width + (h + 1) * d], preferred_element_type=F32)
        k = _rms(k, gk_ref[...])
        k_ref[interleaved, :] = k
        kb_ref[:, cs] = k.astype(BF16)
        v = jnp.dot(xb, w_ref[:, 2 * width + h * d:2 * width + (h + 1) * d], preferred_element_type=F32)
        v_ref[interleaved, :] = v
        vb_ref[:, cs] = v.astype(BF16)
    _project(xb, w_ref, mq_ref, 3 * width, mq_ref.shape[-1])


def _in_proj_sb(x2, gain, w, gq, gk, heads, mem_width, slot, n_slots, kv_all):
    rows, d_model = x2.shape
    d = SB_HEAD_DIM
    width = heads * d
    tm = min(ROW_TILE, rows)
    stack_shape = jax.ShapeDtypeStruct((n_slots, rows * heads, d), F32)
    stack_spec = pl.BlockSpec((None, tm * heads, d), lambda i: (slot, i, 0))
    in_specs = [_row_spec(tm, d_model), _const_spec((1, d_model)), _const_spec(w.shape),
                _const_spec((1, d)), _const_spec((1, d))]
    args = [x2, gain, w, gq, gk]
    aliases = {}
    if kv_all is not None:
        in_specs += [pl.BlockSpec(memory_space=pl.ANY)] * 2
        args += list(kv_all)
        aliases = {5: 4, 6: 5}
    return pl.pallas_call(
        functools.partial(_in_proj_sb_kernel, heads=heads),
        out_shape=[jax.ShapeDtypeStruct((rows, width), BF16)] * 3
        + [jax.ShapeDtypeStruct((rows, mem_width), F32), stack_shape, stack_shape],
        grid=(rows // tm,),
        in_specs=in_specs,
        out_specs=[_row_spec(tm, width)] * 3 + [_row_spec(tm, mem_width), stack_spec, stack_spec],
        input_output_aliases=aliases,
        compiler_params=_params("parallel"),
        name="in_proj_sb",
    )(*args)


def _head_mean_sq(x, hm_ref):
    return _dot_exact_rhs(x * x, hm_ref[...], 3)


def _mem_kv_kernel(mem_ref, g_ref, w_ref, kg_ref, hm_ref, k_ref, v_ref):
    xb = _rms(mem_ref[...], g_ref[...]).astype(BF16)
    width = k_ref.shape[-1]
    k = jnp.dot(xb, w_ref[:, :width], preferred_element_type=F32)
    k_ref[...] = k * lax.rsqrt(_head_mean_sq(k, hm_ref) + NORM_EPS) * kg_ref[...]
    v_ref[...] = jnp.dot(xb, w_ref[:, width:], preferred_element_type=F32)


def _mem_kv(mem, norm_mem, w_kv, k_gain_t, head_mean):
    bp, m, d = mem.shape
    depth = w_kv.shape[0]
    width = w_kv.shape[-1] // 2
    return pl.pallas_call(
        _mem_kv_kernel,
        out_shape=[jax.ShapeDtypeStruct((depth, bp, m, width), F32)] * 2,
        grid=(depth, bp),
        in_specs=[pl.BlockSpec((None, m, d), lambda i, b: (b, 0, 0)),
                  pl.BlockSpec((None, 1, d), lambda i, b: (i, 0, 0)),
                  pl.BlockSpec((None, d, 2 * width), lambda i, b: (i, 0, 0)),
                  pl.BlockSpec((None, 1, width), lambda i, b: (i, 0, 0)),
                  _const_spec((width, width))],
        out_specs=[pl.BlockSpec((None, None, m, width), lambda i, b: (i, b, 0, 0))] * 2,
        compiler_params=_params("parallel", "parallel"),
        name="mem_kv",
    )(mem, norm_mem, w_kv, k_gain_t, head_mean)


def _mix_out_kernel(o_ref, mq_ref, kbd_ref, vbd_ref, qg_ref, hm_ref, w_ref, x_ref, y_ref, *, n_mem):
    q = mq_ref[...]
    qn = q * lax.rsqrt(_head_mean_sq(q, hm_ref) + NORM_EPS) * qg_ref[...]
    s = _dot(qn, kbd_ref[...]) * (MEM_HEAD_DIM ** -0.5)
    heads = s.shape[-1] // n_mem
    mem = None
    for h in range(heads):
        sh = s[:, h * n_mem:(h + 1) * n_mem]
        e = jnp.exp(sh - jnp.max(sh, axis=-1, keepdims=True))
        p = e / jnp.sum(e, axis=-1, keepdims=True)
        part = _dot(p, vbd_ref[h * n_mem:(h + 1) * n_mem, :])
        mem = part if mem is None else mem + part
    w_mix = o_ref.shape[-1]
    y = x_ref[...] + _dot(o_ref[...], w_ref[:w_mix, :]) + _dot(mem, w_ref[w_mix:, :])
    y_ref[...] = y


def _mix_out(o_mix, mq, kbd, vbd, q_gain_t, head_mean, w_out, x):
    b, t, d = x.shape
    w_mix = o_mix.shape[-1]
    mw = mq.shape[-1]
    n_mem = vbd.shape[1] // (mw // MEM_HEAD_DIM)
    tm = min(ROW_TILE, t)
    tok = lambda n: pl.BlockSpec((None, tm, n), lambda bi, i: (bi, i, 0))
    return pl.pallas_call(
        functools.partial(_mix_out_kernel, n_mem=n_mem),
        out_shape=jax.ShapeDtypeStruct((b, t, d), F32),
        grid=(b, t // tm),
        in_specs=[tok(w_mix), tok(mw),
                  pl.BlockSpec((None,) + kbd.shape[1:], lambda bi, i: (bi, 0, 0)),
                  pl.BlockSpec((None,) + vbd.shape[1:], lambda bi, i: (bi, 0, 0)),
                  _const_spec((1, mw)), _const_spec((mw, mw)), _const_spec(w_out.shape), tok(d)],
        out_specs=tok(d),
        compiler_params=_params("parallel", "parallel"),
        name="mix_out",
    )(o_mix, mq, kbd, vbd, q_gain_t, head_mean, w_out, x)


def _ffn_kernel(x_ref, g_ref, wu_ref, wd_ref, y_ref):
    x = x_ref[...]
    hb = _rms(x, g_ref[...]).astype(BF16)
    acc = x
    for c in range(0, wu_ref.shape[-1], FF_CHUNK):
        a = jnp.maximum(jnp.dot(hb, wu_ref[:, c:c + FF_CHUNK], preferred_element_type=F32), 0.0)
        acc = acc + jnp.dot((a * a).astype(BF16), wd_ref[c:c + FF_CHUNK, :], preferred_element_type=F32)
    y_ref[...] = acc


def _ffn(x2, gain, w_up, w_down):
    rows, d = x2.shape
    tm = min(ROW_TILE, rows)
    return pl.pallas_call(
        _ffn_kernel,
        out_shape=jax.ShapeDtypeStruct((rows, d), F32),
        grid=(rows // tm,),
        in_specs=[_row_spec(tm, d), _const_spec((1, d)), _const_spec(w_up.shape), _const_spec(w_down.shape)],
        out_specs=_row_spec(tm, d),
        compiler_params=_params("parallel"),
        name="ffn",
    )(x2, gain, w_up, w_down)


CONV_PAD = SUBLANES


def _cat3(parts):
    hi, lo = parts
    return jnp.concatenate([hi, lo, hi], axis=1)


def _unit_lower_inverse(lower, eye):
    r = lower.shape[0]
    p = -lower
    inv = eye + p
    p_parts = _split_bf16(p, 2)
    rhs = jnp.concatenate([p_parts[0], p_parts[0], p_parts[1]], axis=0)
    p = jnp.dot(_cat3(p_parts), rhs, preferred_element_type=F32)
    span = 4
    while span <= CHUNK:
        p_parts = _split_bf16(p, 2)
        rhs = jnp.concatenate([p_parts[0], p_parts[0], p_parts[1]], axis=0)
        i_cat = _cat3(_split_bf16(inv, 2))
        if span < CHUNK:
            both = jnp.dot(jnp.concatenate([_cat3(p_parts), i_cat], axis=0), rhs, preferred_element_type=F32)
            p = both[:r]
            inv = inv + both[r:]
        else:
            inv = inv + jnp.dot(i_cat, rhs, preferred_element_type=F32)
        span *= 2
    return inv


def _gdn_kernel(qkv_ref, z_ref, ba_ref, buf_ref, s0_ref, cw_ref, nal_ref, dt_ref, og_ref,
                o_ref, nbuf_ref, sout_ref,
                xp_ref, act_ref, s_ref, gc_ref, kwqw_ref, ku_ref, au_ref, *, heads):
    blk = pl.program_id(1)
    tb = qkv_ref.shape[0]
    hd = GDN_HEAD_DIM
    width = heads * hd
    tail = GDN_CONV - 1
    group = min(tb, GDN_GROUP)
    n_sub = group // CHUNK

    @pl.when(blk == 0)
    def _():
        xp_ref[CONV_PAD - tail:CONV_PAD, :] = buf_ref[...]
        s_ref[...] = s0_ref[...]

    xp_ref[CONV_PAD:CONV_PAD + tb, :] = qkv_ref[...]

    for ct in range(3 * heads):
        cs = slice(ct * hd, (ct + 1) * hd)
        y = None
        for j in range(GDN_CONV):
            r0 = CONV_PAD - tail + j
            term = xp_ref[r0:r0 + tb, cs] * cw_ref[j:j + 1, cs]
            y = term if y is None else y + term
        y = y * _sigmoid(y)
        if ct < 2 * heads:
            y = y * lax.rsqrt(jnp.sum(y * y, axis=-1, keepdims=True) + NORM_EPS)
            if ct < heads:
                y = y * (hd ** -0.5)
        act_ref[:, cs] = y

    new_tail = xp_ref[CONV_PAD + tb - tail:CONV_PAD + tb, :]
    xp_ref[CONV_PAD - tail:CONV_PAD, :] = new_tail

    ba = ba_ref[...]
    beta_all = _sigmoid(ba)
    g_all = nal_ref[...] * _softplus(ba + dt_ref[...])

    row = lax.broadcasted_iota(jnp.int32, (group, group), 0)
    col = lax.broadcasted_iota(jnp.int32, (group, group), 1)
    same = (row // CHUNK) == (col // CHUNK)
    causal = jnp.logical_and(same, row >= col)
    strict = jnp.logical_and(same, row > col)
    tri3 = jnp.concatenate([causal.astype(BF16)] * 3, axis=1)
    upper_f = jnp.logical_and(same, row <= col).astype(F32)
    eye = (row == col).astype(F32)
    sub_of_row = lax.broadcasted_iota(jnp.int32, (group, hd), 0) // CHUNK

    for gi in range(tb // group):
        rs = slice(gi * group, (gi + 1) * group)
        g_s = g_all[rs, :]
        gc_all = jnp.dot(tri3, jnp.concatenate(_split_bf16(g_s, 3), axis=0), preferred_element_type=F32)
        gc_ref[rs, :] = gc_all
        for h in range(heads):
            gl = heads + h
            gcol = gc_all[:, gl:gl + 1]
            grow = jnp.sum(g_s[:, gl:gl + 1] * upper_f, axis=0, keepdims=True)
            glast = jnp.concatenate(
                [jnp.broadcast_to(gc_all[(s + 1) * CHUNK - 1:(s + 1) * CHUNK, gl:gl + 1], (CHUNK, 1))
                 for s in range(n_sub)], axis=0)
            decay = jnp.where(causal, jnp.exp(jnp.where(causal, gcol - grow, 0.0)), 0.0)
            bcol = beta_all[rs, h:h + 1]
            q = act_ref[rs, h * hd:(h + 1) * hd]
            k = act_ref[rs, width + h * hd:width + (h + 1) * hd]
            v = act_ref[rs, 2 * width + h * hd:2 * width + (h + 1) * hd]
            kb = k * bcol
            egc = jnp.exp(gcol)
            kq = _dot_nt(jnp.concatenate([kb, q], axis=0), k)
            lower = kq[:group] * jnp.where(strict, decay, 0.0)
            a_intra = kq[group:] * decay
            inv = _unit_lower_inverse(lower, eye)
            sol = _dot(inv, jnp.concatenate([v * bcol, kb * egc], axis=-1)).astype(BF16)
            k_dec = k * jnp.exp(glast - gcol)
            if n_sub > 1:
                k_dec = jnp.concatenate([jnp.where(sub_of_row == s, k_dec, 0.0) for s in range(n_sub)], axis=1)
            kd_sol = _dot_tn(k_dec, sol)
            a_sol = _dot(a_intra, sol)
            qw = q * egc - a_sol[:, hd:]
            for s in range(n_sub):
                idx = (gi * n_sub + s) * heads + h
                ku_ref[idx] = kd_sol[s * hd:(s + 1) * hd, :hd]
                au_ref[idx] = a_sol[s * CHUNK:(s + 1) * CHUNK, :hd]
                kwqw_ref[idx, :hd, :] = kd_sol[s * hd:(s + 1) * hd, hd:].astype(BF16)
                kwqw_ref[idx, hd:, :] = qw[s * CHUNK:(s + 1) * CHUNK, :].astype(BF16)

    for c in range(tb // CHUNK):
        rs = slice(c * CHUNK, (c + 1) * CHUNK)
        for h in range(heads):
            idx = c * heads + h
            gl = heads + h
            s = s_ref[h]
            glast = gc_ref[(c + 1) * CHUNK - 1:(c + 1) * CHUNK, gl:gl + 1]
            r = jnp.dot(kwqw_ref[idx], s.astype(BF16), preferred_element_type=F32)
            s_ref[h] = s * jnp.exp(glast) - r[:hd] + ku_ref[idx]
            o = r[hd:] + au_ref[idx]
            zh = z_ref[rs, h * hd:(h + 1) * hd]
            o_ref[rs, h * hd:(h + 1) * hd] = _rms(o, og_ref[...]) * (zh * _sigmoid(zh))

    @pl.when(blk == pl.num_programs(1) - 1)
    def _():
        nbuf_ref[...] = new_tail
        sout_ref[...] = s_ref[...]


def _gdn_mixer(qkv, z, ba, conv_buf, s0, conv_w, neg_exp_alog, dt_bias, o_gain):
    b, t, cw = qkv.shape
    heads = s0.shape[1]
    hd = GDN_HEAD_DIM
    width = heads * hd
    tail = GDN_CONV - 1
    tb = min(t, GDN_BLOCK)
    n_ch = (tb // CHUNK) * heads
    tok = lambda n: pl.BlockSpec((None, tb, n), lambda bi, i: (bi, i, 0))
    return pl.pallas_call(
        functools.partial(_gdn_kernel, heads=heads),
        out_shape=[jax.ShapeDtypeStruct((b, t, width), F32),
                   jax.ShapeDtypeStruct((b, tail, cw), F32),
                   jax.ShapeDtypeStruct((b, heads, hd, hd), F32)],
        grid=(b, t // tb),
        in_specs=[tok(cw), tok(width), tok(LANES),
                  pl.BlockSpec((None, tail, cw), lambda bi, i: (bi, 0, 0)),
                  pl.BlockSpec((None, heads, hd, hd), lambda bi, i: (bi, 0, 0, 0)),
                  _const_spec((GDN_CONV, cw)), _const_spec((1, LANES)), _const_spec((1, LANES)),
                  _const_spec((1, hd))],
        out_specs=[tok(width),
                   pl.BlockSpec((None, tail, cw), lambda bi, i: (bi, 0, 0)),
                   pl.BlockSpec((None, heads, hd, hd), lambda bi, i: (bi, 0, 0, 0))],
        scratch_shapes=[pltpu.VMEM((CONV_PAD + tb, cw), F32), pltpu.VMEM((tb, cw), F32),
                        pltpu.VMEM((heads, hd, hd), F32), pltpu.VMEM((tb, LANES), F32),
                        pltpu.VMEM((n_ch, hd + CHUNK, hd), BF16), pltpu.VMEM((n_ch, hd, hd), F32),
                        pltpu.VMEM((n_ch, CHUNK, hd), F32)],
        compiler_params=_params("parallel", "arbitrary"),
        name="gdn_mixer",
    )(qkv, z, ba, conv_buf, s0, conv_w, neg_exp_alog, dt_bias, o_gain)


def _suffix_sums(m, tri2):
    return jnp.dot(jnp.concatenate(_split_bf16(m, 2), axis=1), tri2, preferred_element_type=F32)


def _tri2(n):
    row = lax.broadcasted_iota(jnp.int32, (n, n), 0)
    col = lax.broadcasted_iota(jnp.int32, (n, n), 1)
    tri = (row >= col).astype(BF16)
    return jnp.concatenate([tri, tri], axis=0), row > col


def _sb_attn_kernel(q_ref, kd_ref, vd_ref, *refs, heads, tk, off_blocks, interleaved, has_prev):
    if has_prev:
        kp_ref, vp_ref = refs[:2]
        refs = refs[2:]
    ko_ref, vo_ref, gq_ref, gk_ref, o_ref, acc_ref, suf_ref = refs
    tq = q_ref.shape[0]
    d = SB_HEAD_DIM
    n_off = off_blocks(pl.program_id(1))

    tri_q, strict = _tri2(tq)
    tri_k = tri_q if tk == tq else _tri2(tk)[0]
    suf_min = None
    for h in range(heads):
        cs = slice(h * d, (h + 1) * d)
        q = q_ref[:, cs]
        z = _dot_nt(q, kd_ref[:, cs])
        m = jnp.where(strict, jnp.log1p(jnp.exp(z)), 0.0)
        c = _suffix_sums(m, tri_q)
        a = jnp.where(strict, jnp.exp(z - c), 0.0)
        acc = _dot(a, vd_ref[:, cs])
        suf = c[:, 0:1]
        if has_prev:
            valid = n_off > 0
            zp = _dot_nt(q, kp_ref[:, cs])
            cp = _suffix_sums(jnp.log1p(jnp.exp(zp)), tri_k)
            acc = acc + _dot(jnp.where(valid, jnp.exp(zp - cp - suf), 0.0), vp_ref[:, cs])
            suf = suf + jnp.where(valid, cp[:, 0:1], 0.0)
        acc_ref[:, cs] = acc
        suf_ref[h] = suf
        hmin = jnp.min(suf)
        suf_min = hmin if suf_min is None else jnp.minimum(suf_min, hmin)

    z_bound = (jnp.max(jnp.abs(gq_ref[...])) * jnp.max(jnp.abs(gk_ref[...]))
               * (math.sqrt(SB_HEAD_DIM) * 1.01))

    def live(carry):
        n, smin = carry
        return jnp.logical_and(n >= 0, z_bound - smin > -F32_MIN_NORMAL_LOG)

    def step(carry):
        n, _ = carry
        smin = None
        for h in range(heads):
            cs = slice(h * d, (h + 1) * d)
            if interleaved:
                rows = pl.ds(n * (tk * heads) + h, tk, stride=heads)
                kb, vb = ko_ref[rows, :], vo_ref[rows, :]
            else:
                rows = pl.ds(pl.multiple_of(n * tk, tk), tk)
                kb, vb = ko_ref[rows, cs], vo_ref[rows, cs]
            zo = _dot_nt(q_ref[:, cs], kb)
            co = _suffix_sums(jnp.log1p(jnp.exp(zo)), tri_k)
            suf = suf_ref[h]
            acc_ref[:, cs] += _dot(jnp.exp(zo - co - suf), vb)
            suf = suf + co[:, 0:1]
            suf_ref[h] = suf
            hmin = jnp.min(suf)
            smin = hmin if smin is None else jnp.minimum(smin, hmin)
        return n - 1, smin

    lax.while_loop(live, step, (n_off - (2 if has_prev else 1), suf_min))
    o_ref[...] = acc_ref[...]


def _sb_attention(q, k_new, v_new, k_off, v_off, gq, gk, tq, tk, causal_prompt):
    b, t, width = q.shape
    d = SB_HEAD_DIM
    heads = width // d
    blk = pl.BlockSpec((None, tq, width), lambda bi, i: (bi, i, 0))
    full = pl.BlockSpec((None,) + k_off.shape[1:], lambda bi, i: (bi, 0, 0), pipeline_mode=pl.Buffered(1))
    in_specs = [blk, blk, blk]
    args = [q, k_new, v_new]
    if causal_prompt:
        assert tq == tk
        off_blocks = lambda i: i
        prev = pl.BlockSpec((None, tk, width), lambda bi, i: (bi, jnp.maximum(i - 1, 0), 0))
        in_specs += [prev, prev]
        args += [k_off, v_off]
    else:
        t_off = k_off.shape[1] // heads
        off_blocks = lambda i: t_off // tk
    in_specs += [full, full, _const_spec((1, d)), _const_spec((1, d))]
    args += [k_off, v_off, gq, gk]
    return pl.pallas_call(
        functools.partial(_sb_attn_kernel, heads=heads, tk=tk, off_blocks=off_blocks,
                          interleaved=not causal_prompt, has_prev=causal_prompt),
        out_shape=jax.ShapeDtypeStruct((b, t, width), F32),
        grid=(b, t // tq),
        in_specs=in_specs,
        out_specs=blk,
        scratch_shapes=[pltpu.VMEM((tq, width), F32), pltpu.VMEM((heads, tq, 1), F32)],
        compiler_params=_params("parallel", "arbitrary"),
        name="sb_attention",
    )(*args)


def _block_diag_kv(mk, mv):
    b, m, h, dh = mk.shape
    eye = jnp.eye(h, dtype=mk.dtype)
    kbd = jnp.einsum("bmhd,hg->bhdgm", mk, eye).reshape(b, h * dh, h * m)
    vbd = jnp.einsum("bmhd,hg->bhmgd", mv, eye).reshape(b, h * m, h * dh)
    return kbd.astype(BF16), vbd.astype(BF16)


def _head_mean_matrix(width, head_dim):
    idx = jnp.arange(width) // head_dim
    return ((idx[:, None] == idx[None, :]).astype(F32) / head_dim).astype(BF16)


def kernel(x_prompt, x_sample, state_gdn_conv, state_gdn_s, cache_sb_k, cache_sb_v, cache_mem_k, cache_mem_v,
           mem_prompt, norm_mix, norm_mem, norm_ffn, w_in_gdn, w_in_sb, w_mem_kv, mem_q_gain, mem_k_gain,
           gdn_conv_w, gdn_a_log, gdn_dt_bias, gdn_o_gain, sb_q_gain, sb_k_gain, w_out_gdn, w_out_sb, w_up, w_down):
    depth, d_model = norm_mix.shape
    bp, t_p, _ = x_prompt.shape
    bs, t_s, _ = x_sample.shape
    gdn_heads = gdn_a_log.shape[1]
    gdn_width = gdn_heads * GDN_HEAD_DIM
    conv_ch = gdn_conv_w.shape[-1]
    n_sb, _, past, sb_heads, _ = cache_sb_k.shape
    sb_width = sb_heads * SB_HEAD_DIM
    mem_tokens, mem_heads = cache_mem_k.shape[2], cache_mem_k.shape[3]
    mem_width = mem_heads * MEM_HEAD_DIM
    assert MEM_HEAD_DIM & (MEM_HEAD_DIM - 1) == 0

    n_gate = 2 * gdn_heads
    tok_in = conv_ch + gdn_width + n_gate
    w_g = w_in_gdn
    w_gdn = jnp.concatenate(
        [w_g[..., :conv_ch + gdn_width],
         jnp.pad(w_g[..., conv_ch + gdn_width:tok_in], ((0, 0), (0, 0), (0, LANES - n_gate))),
         w_g[..., tok_in:]], axis=-1).astype(BF16)
    gdn_widths = (conv_ch, gdn_width, LANES, mem_width)
    w_sb = w_in_sb.astype(BF16)
    w_kv = w_mem_kv.astype(BF16)
    w_og = w_out_gdn.astype(BF16)
    w_os = w_out_sb.astype(BF16)
    w_u = w_up.astype(BF16)
    w_d = w_down.astype(BF16)

    head_mean = _head_mean_matrix(mem_width, MEM_HEAD_DIM)
    q_gain_t = jnp.tile(mem_q_gain, (1, mem_heads))[:, None, :]
    k_gain_t = jnp.tile(mem_k_gain, (1, mem_heads))[:, None, :]
    lane_pad = lambda a: jnp.pad(a, ((0, 0), (gdn_heads, LANES - n_gate)))[:, None, :]
    neg_exp_alog = lane_pad(-jnp.exp(gdn_a_log))
    dt_bias = lane_pad(gdn_dt_bias)

    pmk, pmv = _mem_kv(mem_prompt, norm_mem[:, None, :], w_kv, k_gain_t, head_mean)
    pmk5 = pmk.reshape(depth, bp, mem_tokens, mem_heads, MEM_HEAD_DIM)
    pmv5 = pmv.reshape(depth, bp, mem_tokens, mem_heads, MEM_HEAD_DIM)

    yp, ys = x_prompt, x_sample
    pc, pst, sc, sst = [], [], [], []
    kv_stacks = {True: None, False: None}
    for i in range(depth):
        j = i // 2
        g_mix = norm_mix[i][None, :]
        paths = []
        for y, cmk, cmv, is_prompt in ((yp, pmk5[i], pmv5[i], True), (ys, cache_mem_k[i], cache_mem_v[i], False)):
            b, t, _ = y.shape
            y2 = y.reshape(b * t, d_model)
            kbd, vbd = _block_diag_kv(cmk, cmv)
            if i % 2 == 0:
                qkv, z, ba, mq = _in_proj_gdn(y2, g_mix, w_gdn[j], gdn_widths)
                if is_prompt:
                    buf0 = jnp.zeros((b, GDN_CONV - 1, conv_ch), F32)
                    st0 = jnp.zeros((b, gdn_heads, GDN_HEAD_DIM, GDN_HEAD_DIM), F32)
                else:
                    buf0, st0 = state_gdn_conv[j], state_gdn_s[j]
                o_mix, nbuf, s_new = _gdn_mixer(
                    qkv.reshape(b, t, conv_ch), z.reshape(b, t, gdn_width), ba.reshape(b, t, LANES),
                    buf0, st0, gdn_conv_w[j], neg_exp_alog[j], dt_bias[j], gdn_o_gain[j][None, :])
                (pc if is_prompt else sc).append(nbuf)
                (pst if is_prompt else sst).append(s_new)
                w_o = w_og[j]
            else:
                gq, gk = sb_q_gain[j][None, :], sb_k_gain[j][None, :]
                q, kb, vb, mq, k_all, v_all = _in_proj_sb(y2, g_mix, w_sb[j], gq, gk, sb_heads, mem_width,
                                                          j, n_sb, kv_stacks[is_prompt])
                kv_stacks[is_prompt] = (k_all, v_all)
                r3 = lambda a: a.reshape(b, t, sb_width)
                if is_prompt:
                    tq = min(t, SB_QUERY_BLOCK)
                    o_mix = _sb_attention(r3(q), r3(kb), r3(vb), r3(kb), r3(vb), gq, gk, tq, tq, True)
                else:
                    o_mix = _sb_attention(r3(q), r3(kb), r3(vb),
                                          cache_sb_k[j].reshape(b, past * sb_heads, SB_HEAD_DIM),
                                          cache_sb_v[j].reshape(b, past * sb_heads, SB_HEAD_DIM),
                                          gq, gk, t, min(past, SB_CACHE_BLOCK), False)
                w_o = w_os[j]
            y = _mix_out(o_mix, mq.reshape(b, t, mem_width), kbd, vbd, q_gain_t[i], head_mean, w_o, y)
            y = _ffn(y.reshape(b * t, d_model), norm_ffn[i][None, :], w_u[i], w_d[i]).reshape(b, t, d_model)
            paths.append(y)
        yp, ys = paths
    heads5 = lambda a, b, t: a.reshape(n_sb, b, t, sb_heads, SB_HEAD_DIM)
    pk, pv = (heads5(a, bp, t_p) for a in kv_stacks[True])
    sk, sv = (heads5(a, bs, t_s) for a in kv_stacks[False])
    return (yp, ys, jnp.stack(pc), jnp.stack(pst), pk, pv, pmk5, pmv5,
            jnp.stack(sc), jnp.stack(sst), sk, sv)
```

```python
import functools
import math

import jax
import jax.numpy as jnp
from jax import lax
from jax.experimental import pallas as pl
from jax.experimental.pallas import tpu as pltpu

F32 = jnp.float32
BF16 = jnp.bfloat16

NORM_EPS = 1e-6
CHUNK = 64
GDN_HEAD_DIM = 128
GDN_CONV = 4
GDN_BLOCK = 256
GDN_GROUP = 128
SB_HEAD_DIM = 128
SB_QUERY_BLOCK = 256
SB_CACHE_BLOCK = 128
MEM_HEAD_DIM = 64
LANES = 128
SUBLANES = 8
VMEM_LIMIT_BYTES = 56 * 1024 * 1024
ROW_TILE = 512
COL_CHUNK = 512
FF_CHUNK = 1024
F32_MIN_NORMAL_LOG = 88.0


def _params(*semantics):
    return pltpu.CompilerParams(dimension_semantics=semantics, vmem_limit_bytes=VMEM_LIMIT_BYTES)


def _dot(a, b):
    return jnp.dot(a.astype(BF16), b.astype(BF16), preferred_element_type=F32)


def _dot_nt(a, b):
    return lax.dot_general(a.astype(BF16), b.astype(BF16), (((1,), (1,)), ((), ())),
                           preferred_element_type=F32)


def _dot_tn(a, b):
    return lax.dot_general(a.astype(BF16), b.astype(BF16), (((0,), (0,)), ((), ())),
                           preferred_element_type=F32)


def _split_bf16(x, terms):
    parts = []
    for _ in range(terms):
        p = x.astype(BF16)
        parts.append(p)
        x = x - p.astype(F32)
    return parts


def _dot_exact_rhs(a, b_exact, terms):
    acc = None
    for p in _split_bf16(a, terms):
        t = jnp.dot(p, b_exact, preferred_element_type=F32)
        acc = t if acc is None else acc + t
    return acc


def _rms(x, gain):
    return x * lax.rsqrt(jnp.mean(x * x, axis=-1, keepdims=True) + NORM_EPS) * gain


def _sigmoid(x):
    return 1.0 / (1.0 + jnp.exp(-x))


def _softplus(x):
    return jnp.maximum(x, 0.0) + jnp.log1p(jnp.exp(-jnp.abs(x)))


def _const_spec(shape):
    zeros = (0,) * len(shape)
    return pl.BlockSpec(shape, lambda *_: zeros)


def _row_spec(tm, n):
    return pl.BlockSpec((tm, n), lambda i: (i, 0))


def _layer_spec(stacked, layer):
    zeros = (0,) * (stacked.ndim - 1)
    return pl.BlockSpec((None,) + stacked.shape[1:], lambda *_: (layer,) + zeros)


def _project(xb, w_ref, o_ref, off, n):
    for c in range(0, n, COL_CHUNK):
        cw = min(COL_CHUNK, n - c)
        o_ref[:, c:c + cw] = jnp.dot(xb, w_ref[:, off + c:off + c + cw],
                                     preferred_element_type=F32).astype(o_ref.dtype)


def _in_proj_gdn_kernel(x_ref, g_ref, w_ref, qkv_ref, z_ref, ba_ref, mq_ref):
    xb = _rms(x_ref[...], g_ref[...]).astype(BF16)
    off = 0
    for o_ref in (qkv_ref, z_ref, ba_ref, mq_ref):
        n = o_ref.shape[-1]
        _project(xb, w_ref, o_ref, off, n)
        off += n


def _in_proj_gdn(x2, gain, w, layer, widths):
    rows, d = x2.shape
    tm = min(ROW_TILE, rows)
    return pl.pallas_call(
        _in_proj_gdn_kernel,
        out_shape=[jax.ShapeDtypeStruct((rows, n), F32) for n in widths],
        grid=(rows // tm,),
        in_specs=[_row_spec(tm, d), _const_spec((1, d)), _layer_spec(w, layer)],
        out_specs=[_row_spec(tm, n) for n in widths],
        compiler_params=_params("parallel"),
        name="in_proj_gdn",
    )(x2, gain, w)


def _in_proj_sb_kernel(x_ref, g_ref, w_ref, gq_ref, gk_ref, *refs, heads):
    q_ref, kb_ref, vb_ref, mq_ref, k_ref, v_ref = refs[-6:]
    xb = _rms(x_ref[...], g_ref[...]).astype(BF16)
    tm = x_ref.shape[0]
    d = SB_HEAD_DIM
    width = heads * d
    for h in range(heads):
        cs = slice(h * d, (h + 1) * d)
        interleaved = pl.ds(h, tm, stride=heads)
        q = jnp.dot(xb, w_ref[:, h * d:(h + 1) * d], preferred_element_type=F32)
        q_ref[:, cs] = (_rms(q, gq_ref[...]) * (d ** -0.5)).astype(BF16)
        k = jnp.dot(xb, w_ref[:, width + h * d:width + (h + 1) * d], preferred_element_type=F32)
        k = _rms(k, gk_ref[...])
        k_ref[interleaved, :] = k
        kb_ref[:, cs] = k.astype(BF16)
        v = jnp.dot(xb, w_ref[:, 2 * width + h * d:2 * width + (h + 1) * d], preferred_element_type=F32)
        v_ref[interleaved, :] = v
        vb_ref[:, cs] = v.astype(BF16)
    _project(xb, w_ref, mq_ref, 3 * width, mq_ref.shape[-1])


def _in_proj_sb(x2, gain, w, gq, gk, heads, mem_width, slot, n_slots, kv_all):
    rows, d_model = x2.shape
    d = SB_HEAD_DIM
    width = heads * d
    tm = min(ROW_TILE, rows)
    stack_shape = jax.ShapeDtypeStruct((n_slots, rows * heads, d), F32)
    stack_spec = pl.BlockSpec((None, tm * heads, d), lambda i: (slot, i, 0))
    in_specs = [_row_spec(tm, d_model), _const_spec((1, d_model)), _layer_spec(w, slot),
                _const_spec((1, d)), _const_spec((1, d))]
    args = [x2, gain, w, gq, gk]
    aliases = {}
    if kv_all is not None:
        in_specs += [pl.BlockSpec(memory_space=pl.ANY)] * 2
        args += list(kv_all)
        aliases = {5: 4, 6: 5}
    return pl.pallas_call(
        functools.partial(_in_proj_sb_kernel, heads=heads),
        out_shape=[jax.ShapeDtypeStruct((rows, width), BF16)] * 3
        + [jax.ShapeDtypeStruct((rows, mem_width), F32), stack_shape, stack_shape],
        grid=(rows // tm,),
        in_specs=in_specs,
        out_specs=[_row_spec(tm, width)] * 3 + [_row_spec(tm, mem_width), stack_spec, stack_spec],
        input_output_aliases=aliases,
        compiler_params=_params("parallel"),
        name="in_proj_sb",
    )(*args)


def _head_mean_sq(x, hm_ref):
    return _dot_exact_rhs(x * x, hm_ref[...], 3)


def _mem_kv_kernel(mem_ref, g_ref, w_ref, kg_ref, hm_ref, k_ref, v_ref):
    xb = _rms(mem_ref[...], g_ref[...]).astype(BF16)
    width = k_ref.shape[-1]
    k = jnp.dot(xb, w_ref[:, :width], preferred_element_type=F32)
    k_ref[...] = k * lax.rsqrt(_head_mean_sq(k, hm_ref) + NORM_EPS) * kg_ref[...]
    v_ref[...] = jnp.dot(xb, w_ref[:, width:], preferred_element_type=F32)


def _mem_kv(mem, norm_mem, w_kv, k_gain_t, head_mean):
    bp, m, d = mem.shape
    depth = w_kv.shape[0]
    width = w_kv.shape[-1] // 2
    return pl.pallas_call(
        _mem_kv_kernel,
        out_shape=[jax.ShapeDtypeStruct((depth, bp, m, width), F32)] * 2,
        grid=(depth, bp),
        in_specs=[pl.BlockSpec((None, m, d), lambda i, b: (b, 0, 0)),
                  pl.BlockSpec((None, 1, d), lambda i, b: (i, 0, 0)),
                  pl.BlockSpec((None, d, 2 * width), lambda i, b: (i, 0, 0)),
                  pl.BlockSpec((None, 1, width), lambda i, b: (i, 0, 0)),
                  _const_spec((width, width))],
        out_specs=[pl.BlockSpec((None, None, m, width), lambda i, b: (i, b, 0, 0))] * 2,
        compiler_params=_params("parallel", "parallel"),
        name="mem_kv",
    )(mem, norm_mem, w_kv, k_gain_t, head_mean)


def _mix_out_kernel(o_ref, mq_ref, k_ref, v_ref, qg_ref, hm_ref, w_ref, x_ref, y_ref):
    q = mq_ref[...]
    mw = q.shape[-1]
    qn = q * lax.rsqrt(_head_mean_sq(q, hm_ref) + NORM_EPS) * qg_ref[...]
    kb = k_ref[...].astype(BF16)
    v = v_ref[...]
    lane_head = lax.broadcasted_iota(jnp.int32, (1, mw), 1) // MEM_HEAD_DIM
    scores = [_dot_nt(jnp.where(lane_head == h, qn, 0.0), kb) * (MEM_HEAD_DIM ** -0.5)
              for h in range(mw // MEM_HEAD_DIM)]
    mem = None
    for h, sh in enumerate(scores):
        e = jnp.exp(sh - jnp.max(sh, axis=-1, keepdims=True))
        p = e / jnp.sum(e, axis=-1, keepdims=True)
        part = _dot(p, jnp.where(lane_head == h, v, 0.0))
        mem = part if mem is None else mem + part
    w_mix = o_ref.shape[-1]
    y_ref[...] = x_ref[...] + _dot(o_ref[...], w_ref[:w_mix, :]) + _dot(mem, w_ref[w_mix:, :])


def _mix_out(o_mix, mq, mk, mv, layer, q_gain_t, head_mean, w_out, w_layer, x):
    b, t, d = x.shape
    w_mix = o_mix.shape[-1]
    mw = mq.shape[-1]
    tm = min(ROW_TILE, t)
    tok = lambda n: pl.BlockSpec((None, tm, n), lambda bi, i: (bi, i, 0))
    kv = pl.BlockSpec((None, None) + mk.shape[2:], lambda bi, i: (layer, bi, 0, 0))
    return pl.pallas_call(
        _mix_out_kernel,
        out_shape=jax.ShapeDtypeStruct((b, t, d), F32),
        grid=(b, t // tm),
        in_specs=[tok(w_mix), tok(mw), kv, kv, _layer_spec(q_gain_t, layer), _const_spec((mw, mw)),
                  _layer_spec(w_out, w_layer), tok(d)],
        out_specs=tok(d),
        compiler_params=_params("parallel", "parallel"),
        name="mix_out",
    )(o_mix, mq, mk, mv, q_gain_t, head_mean, w_out, x)


def _ffn_kernel(x_ref, g_ref, wu_ref, wd_ref, y_ref):
    x = x_ref[...]
    hb = _rms(x, g_ref[...]).astype(BF16)
    acc = x
    for c in range(0, wu_ref.shape[-1], FF_CHUNK):
        a = jnp.maximum(jnp.dot(hb, wu_ref[:, c:c + FF_CHUNK], preferred_element_type=F32), 0.0)
        acc = acc + jnp.dot((a * a).astype(BF16), wd_ref[c:c + FF_CHUNK, :], preferred_element_type=F32)
    y_ref[...] = acc


def _ffn(x2, gain, w_up, w_down, layer):
    rows, d = x2.shape
    tm = min(ROW_TILE, rows)
    return pl.pallas_call(
        _ffn_kernel,
        out_shape=jax.ShapeDtypeStruct((rows, d), F32),
        grid=(rows // tm,),
        in_specs=[_row_spec(tm, d), _const_spec((1, d)), _layer_spec(w_up, layer), _layer_spec(w_down, layer)],
        out_specs=_row_spec(tm, d),
        compiler_params=_params("parallel"),
        name="ffn",
    )(x2, gain, w_up, w_down)


CONV_PAD = SUBLANES


def _cat3(parts):
    hi, lo = parts
    return jnp.concatenate([hi, lo, hi], axis=1)


def _unit_lower_inverses(lowers, eye):
    r = eye.shape[0]
    ps = [-lower for lower in lowers]
    invs = [eye + p for p in ps]

    def split_rhs(p):
        parts = _split_bf16(p, 2)
        return parts, jnp.concatenate([parts[0], parts[0], parts[1]], axis=0)

    ps = [jnp.dot(_cat3(parts), rhs, preferred_element_type=F32) for parts, rhs in map(split_rhs, ps)]
    span = 4
    while span <= CHUNK:
        nxt_p, nxt_inv = [], []
        for p, inv in zip(ps, invs):
            parts, rhs = split_rhs(p)
            i_cat = _cat3(_split_bf16(inv, 2))
            if span < CHUNK:
                both = jnp.dot(jnp.concatenate([_cat3(parts), i_cat], axis=0), rhs, preferred_element_type=F32)
                nxt_p.append(both[:r])
                nxt_inv.append(inv + both[r:])
            else:
                nxt_inv.append(inv + jnp.dot(i_cat, rhs, preferred_element_type=F32))
        ps, invs = nxt_p, nxt_inv
        span *= 2
    return invs


def _gdn_kernel(qkv_ref, z_ref, ba_ref, buf_ref, s0_ref, cw_ref, nal_ref, dt_ref, og_ref,
                o_ref, nbuf_ref, sout_ref,
                xp_ref, act_ref, s_ref, gc_ref, kwqw_ref, ku_ref, au_ref, *, heads):
    blk = pl.program_id(1)
    tb = qkv_ref.shape[0]
    hd = GDN_HEAD_DIM
    width = heads * hd
    tail = GDN_CONV - 1
    group = min(tb, GDN_GROUP)
    n_sub = group // CHUNK

    @pl.when(blk == 0)
    def _():
        xp_ref[CONV_PAD - tail:CONV_PAD, :] = buf_ref[...]
        s_ref[...] = s0_ref[...]

    xp_ref[CONV_PAD:CONV_PAD + tb, :] = qkv_ref[...]

    for ct in range(3 * heads):
        cs = slice(ct * hd, (ct + 1) * hd)
        y = None
        for j in range(GDN_CONV):
            r0 = CONV_PAD - tail + j
            term = xp_ref[r0:r0 + tb, cs] * cw_ref[j:j + 1, cs]
            y = term if y is None else y + term
        y = y * _sigmoid(y)
        if ct < 2 * heads:
            y = y * lax.rsqrt(jnp.sum(y * y, axis=-1, keepdims=True) + NORM_EPS)
            if ct < heads:
                y = y * (hd ** -0.5)
        act_ref[:, cs] = y

    new_tail = xp_ref[CONV_PAD + tb - tail:CONV_PAD + tb, :]
    xp_ref[CONV_PAD - tail:CONV_PAD, :] = new_tail

    ba = ba_ref[...]
    beta_all = _sigmoid(ba)
    g_all = nal_ref[...] * _softplus(ba + dt_ref[...])

    row = lax.broadcasted_iota(jnp.int32, (group, group), 0)
    col = lax.broadcasted_iota(jnp.int32, (group, group), 1)
    same = (row // CHUNK) == (col // CHUNK)
    causal = jnp.logical_and(same, row >= col)
    strict = jnp.logical_and(same, row > col)
    tri3 = jnp.concatenate([causal.astype(BF16)] * 3, axis=1)
    upper_f = jnp.logical_and(same, row <= col).astype(F32)
    eye = (row == col).astype(F32)
    sub_of_row = lax.broadcasted_iota(jnp.int32, (group, hd), 0) // CHUNK

    pairs = [(gi, h) for gi in range(tb // group) for h in range(heads)]
    gcs = []
    for gi in range(tb // group):
        rs = slice(gi * group, (gi + 1) * group)
        gc = jnp.dot(tri3, jnp.concatenate(_split_bf16(g_all[rs, :], 3), axis=0), preferred_element_type=F32)
        gc_ref[rs, :] = gc
        gcs.append(gc)

    def head_slices(gi, h):
        rs = slice(gi * group, (gi + 1) * group)
        return (act_ref[rs, h * hd:(h + 1) * hd], act_ref[rs, width + h * hd:width + (h + 1) * hd],
                act_ref[rs, 2 * width + h * hd:2 * width + (h + 1) * hd])

    def gate_columns(gi, h):
        gcol = gcs[gi][:, heads + h:heads + h + 1]
        bcol = beta_all[gi * group:(gi + 1) * group, h:h + 1]
        return gcol, bcol

    lowers, a_intras = [], []
    for gi, h in pairs:
        rs = slice(gi * group, (gi + 1) * group)
        gcol, bcol = gate_columns(gi, h)
        grow = jnp.sum(g_all[rs, heads + h:heads + h + 1] * upper_f, axis=0, keepdims=True)
        decay = jnp.where(causal, jnp.exp(jnp.where(causal, gcol - grow, 0.0)), 0.0)
        q, k, _ = head_slices(gi, h)
        kq = _dot_nt(jnp.concatenate([k * bcol, q], axis=0), k)
        lowers.append(kq[:group] * jnp.where(strict, decay, 0.0))
        a_intras.append(kq[group:] * decay)

    invs = _unit_lower_inverses(lowers, eye)

    sols = []
    for (gi, h), inv in zip(pairs, invs):
        gcol, bcol = gate_columns(gi, h)
        _, k, v = head_slices(gi, h)
        sols.append(_dot(inv, jnp.concatenate([v * bcol, k * bcol * jnp.exp(gcol)], axis=-1)).astype(BF16))

    for (gi, h), sol, a_intra in zip(pairs, sols, a_intras):
        gcol, _ = gate_columns(gi, h)
        glast = jnp.concatenate(
            [jnp.broadcast_to(gcol[(s + 1) * CHUNK - 1:(s + 1) * CHUNK, :], (CHUNK, 1)) for s in range(n_sub)],
            axis=0)
        q, k, _ = head_slices(gi, h)
        k_dec = k * jnp.exp(glast - gcol)
        if n_sub > 1:
            k_dec = jnp.concatenate([jnp.where(sub_of_row == s, k_dec, 0.0) for s in range(n_sub)], axis=1)
        kd_sol = _dot_tn(k_dec, sol)
        a_sol = _dot(a_intra, sol)
        qw = q * jnp.exp(gcol) - a_sol[:, hd:]
        for s in range(n_sub):
            idx = (gi * n_sub + s) * heads + h
            ku_ref[idx] = kd_sol[s * hd:(s + 1) * hd, :hd]
            au_ref[idx] = a_sol[s * CHUNK:(s + 1) * CHUNK, :hd]
            kwqw_ref[idx, :hd, :] = kd_sol[s * hd:(s + 1) * hd, hd:].astype(BF16)
            kwqw_ref[idx, hd:, :] = qw[s * CHUNK:(s + 1) * CHUNK, :].astype(BF16)

    for c in range(tb // CHUNK):
        rs = slice(c * CHUNK, (c + 1) * CHUNK)
        for h in range(heads):
            idx = c * heads + h
            gl = heads + h
            s = s_ref[h]
            glast = gc_ref[(c + 1) * CHUNK - 1:(c + 1) * CHUNK, gl:gl + 1]
            r = jnp.dot(kwqw_ref[idx], s.astype(BF16), preferred_element_type=F32)
            s_ref[h] = s * jnp.exp(glast) - r[:hd] + ku_ref[idx]
            o = r[hd:] + au_ref[idx]
            zh = z_ref[rs, h * hd:(h + 1) * hd]
            o_ref[rs, h * hd:(h + 1) * hd] = _rms(o, og_ref[...]) * (zh * _sigmoid(zh))

    @pl.when(blk == pl.num_programs(1) - 1)
    def _():
        nbuf_ref[...] = new_tail
        sout_ref[...] = s_ref[...]


def _gdn_mixer(qkv, z, ba, conv_buf, s0, conv_w, neg_exp_alog, dt_bias, o_gain):
    b, t, cw = qkv.shape
    heads = s0.shape[1]
    hd = GDN_HEAD_DIM
    width = heads * hd
    tail = GDN_CONV - 1
    tb = min(t, GDN_BLOCK)
    n_ch = (tb // CHUNK) * heads
    tok = lambda n: pl.BlockSpec((None, tb, n), lambda bi, i: (bi, i, 0))
    return pl.pallas_call(
        functools.partial(_gdn_kernel, heads=heads),
        out_shape=[jax.ShapeDtypeStruct((b, t, width), F32),
                   jax.ShapeDtypeStruct((b, tail, cw), F32),
                   jax.ShapeDtypeStruct((b, heads, hd, hd), F32)],
        grid=(b, t // tb),
        in_specs=[tok(cw), tok(width), tok(LANES),
                  pl.BlockSpec((None, tail, cw), lambda bi, i: (bi, 0, 0)),
                  pl.BlockSpec((None, heads, hd, hd), lambda bi, i: (bi, 0, 0, 0)),
                  _const_spec((GDN_CONV, cw)), _const_spec((1, LANES)), _const_spec((1, LANES)),
                  _const_spec((1, hd))],
        out_specs=[tok(width),
                   pl.BlockSpec((None, tail, cw), lambda bi, i: (bi, 0, 0)),
                   pl.BlockSpec((None, heads, hd, hd), lambda bi, i: (bi, 0, 0, 0))],
        scratch_shapes=[pltpu.VMEM((CONV_PAD + tb, cw), F32), pltpu.VMEM((tb, cw), F32),
                        pltpu.VMEM((heads, hd, hd), F32), pltpu.VMEM((tb, LANES), F32),
                        pltpu.VMEM((n_ch, hd + CHUNK, hd), BF16), pltpu.VMEM((n_ch, hd, hd), F32),
                        pltpu.VMEM((n_ch, CHUNK, hd), F32)],
        compiler_params=_params("parallel", "arbitrary"),
        name="gdn_mixer",
    )(qkv, z, ba, conv_buf, s0, conv_w, neg_exp_alog, dt_bias, o_gain)


def _suffix_sums(m, tri2):
    return jnp.dot(jnp.concatenate(_split_bf16(m, 2), axis=1), tri2, preferred_element_type=F32)


def _tri2(n):
    row = lax.broadcasted_iota(jnp.int32, (n, n), 0)
    col = lax.broadcasted_iota(jnp.int32, (n, n), 1)
    tri = (row >= col).astype(BF16)
    return jnp.concatenate([tri, tri], axis=0), row > col


def _sb_attn_kernel(q_ref, kd_ref, vd_ref, *refs, heads, tk, off_blocks, interleaved, has_prev):
    if has_prev:
        kp_ref, vp_ref = refs[:2]
        refs = refs[2:]
    ko_ref, vo_ref, gq_ref, gk_ref, o_ref, acc_ref, suf_ref = refs
    tq = q_ref.shape[0]
    d = SB_HEAD_DIM
    n_off = off_blocks(pl.program_id(1))

    tri_q, strict = _tri2(tq)
    tri_k = tri_q if tk == tq else _tri2(tk)[0]
    cols = [slice(h * d, (h + 1) * d) for h in range(heads)]
    zs = [_dot_nt(q_ref[:, cs], kd_ref[:, cs]) for cs in cols]
    cds = [_suffix_sums(jnp.where(strict, jnp.log1p(jnp.exp(z)), 0.0), tri_q) for z in zs]
    sufs = [c[:, 0:1] for c in cds]
    weights = [jnp.where(strict, jnp.exp(z - c), 0.0) for z, c in zip(zs, cds)]
    if has_prev:
        valid = n_off > 0
        zps = [_dot_nt(q_ref[:, cs], kp_ref[:, cs]) for cs in cols]
        cps = [_suffix_sums(jnp.log1p(jnp.exp(zp)), tri_k) for zp in zps]
        weights_p = [jnp.where(valid, jnp.exp(zp - cp - suf), 0.0) for zp, cp, suf in zip(zps, cps, sufs)]
        sufs = [suf + jnp.where(valid, cp[:, 0:1], 0.0) for suf, cp in zip(sufs, cps)]
    suf_min = None
    for h, cs in enumerate(cols):
        acc = _dot(weights[h], vd_ref[:, cs])
        if has_prev:
            acc = acc + _dot(weights_p[h], vp_ref[:, cs])
        acc_ref[:, cs] = acc
        suf_ref[h] = sufs[h]
        hmin = jnp.min(sufs[h])
        suf_min = hmin if suf_min is None else jnp.minimum(suf_min, hmin)

    z_bound = (jnp.max(jnp.abs(gq_ref[...])) * jnp.max(jnp.abs(gk_ref[...]))
               * (math.sqrt(SB_HEAD_DIM) * 1.01))

    def live(carry):
        n, smin = carry
        return jnp.logical_and(n >= 0, z_bound - smin > -F32_MIN_NORMAL_LOG)

    def step(carry):
        n, _ = carry
        kvs = []
        for h, cs in enumerate(cols):
            if interleaved:
                rows = pl.ds(n * (tk * heads) + h, tk, stride=heads)
                kvs.append((ko_ref[rows, :], vo_ref[rows, :]))
            else:
                rows = pl.ds(pl.multiple_of(n * tk, tk), tk)
                kvs.append((ko_ref[rows, cs], vo_ref[rows, cs]))
        zos = [_dot_nt(q_ref[:, cs], kb) for cs, (kb, _) in zip(cols, kvs)]
        cos = [_suffix_sums(jnp.log1p(jnp.exp(zo)), tri_k) for zo in zos]
        smin = None
        for h, cs in enumerate(cols):
            suf = suf_ref[h]
            acc_ref[:, cs] += _dot(jnp.exp(zos[h] - cos[h] - suf), kvs[h][1])
            suf = suf + cos[h][:, 0:1]
            suf_ref[h] = suf
            hmin = jnp.min(suf)
            smin = hmin if smin is None else jnp.minimum(smin, hmin)
        return n - 1, smin

    lax.while_loop(live, step, (n_off - (2 if has_prev else 1), suf_min))
    o_ref[...] = acc_ref[...]


def _sb_attention(q, k_new, v_new, k_off, v_off, gq, gk, tq, tk, causal_prompt, layer=0):
    b, t, width = q.shape
    d = SB_HEAD_DIM
    heads = width // d
    blk = pl.BlockSpec((None, tq, width), lambda bi, i: (bi, i, 0))
    if causal_prompt:
        full = pl.BlockSpec((None,) + k_off.shape[1:], lambda bi, i: (bi, 0, 0), pipeline_mode=pl.Buffered(1))
    else:
        full = pl.BlockSpec((None, None) + k_off.shape[2:], lambda bi, i: (layer, bi, 0, 0),
                            pipeline_mode=pl.Buffered(1))
    in_specs = [blk, blk, blk]
    args = [q, k_new, v_new]
    if causal_prompt:
        assert tq == tk
        off_blocks = lambda i: i
        prev = pl.BlockSpec((None, tk, width), lambda bi, i: (bi, jnp.maximum(i - 1, 0), 0))
        in_specs += [prev, prev]
        args += [k_off, v_off]
    else:
        t_off = k_off.shape[2] // heads
        off_blocks = lambda i: t_off // tk
    in_specs += [full, full, _const_spec((1, d)), _const_spec((1, d))]
    args += [k_off, v_off, gq, gk]
    return pl.pallas_call(
        functools.partial(_sb_attn_kernel, heads=heads, tk=tk, off_blocks=off_blocks,
                          interleaved=not causal_prompt, has_prev=causal_prompt),
        out_shape=jax.ShapeDtypeStruct((b, t, width), F32),
        grid=(b, t // tq),
        in_specs=in_specs,
        out_specs=blk,
        scratch_shapes=[pltpu.VMEM((tq, width), F32), pltpu.VMEM((heads, tq, 1), F32)],
        compiler_params=_params("parallel", "arbitrary"),
        name="sb_attention",
    )(*args)


def _head_mean_matrix(width, head_dim):
    idx = jnp.arange(width) // head_dim
    return ((idx[:, None] == idx[None, :]).astype(F32) / head_dim).astype(BF16)


def kernel(x_prompt, x_sample, state_gdn_conv, state_gdn_s, cache_sb_k, cache_sb_v, cache_mem_k, cache_mem_v,
           mem_prompt, norm_mix, norm_mem, norm_ffn, w_in_gdn, w_in_sb, w_mem_kv, mem_q_gain, mem_k_gain,
           gdn_conv_w, gdn_a_log, gdn_dt_bias, gdn_o_gain, sb_q_gain, sb_k_gain, w_out_gdn, w_out_sb, w_up, w_down):
    depth, d_model = norm_mix.shape
    bp, t_p, _ = x_prompt.shape
    bs, t_s, _ = x_sample.shape
    gdn_heads = gdn_a_log.shape[1]
    gdn_width = gdn_heads * GDN_HEAD_DIM
    conv_ch = gdn_conv_w.shape[-1]
    n_sb, _, past, sb_heads, _ = cache_sb_k.shape
    sb_width = sb_heads * SB_HEAD_DIM
    mem_tokens, mem_heads = cache_mem_k.shape[2], cache_mem_k.shape[3]
    mem_width = mem_heads * MEM_HEAD_DIM
    assert MEM_HEAD_DIM & (MEM_HEAD_DIM - 1) == 0

    n_gate = 2 * gdn_heads
    tok_in = conv_ch + gdn_width + n_gate
    w_g = w_in_gdn
    w_gdn = jnp.concatenate(
        [w_g[..., :conv_ch + gdn_width],
         jnp.pad(w_g[..., conv_ch + gdn_width:tok_in], ((0, 0), (0, 0), (0, LANES - n_gate))),
         w_g[..., tok_in:]], axis=-1).astype(BF16)
    gdn_widths = (conv_ch, gdn_width, LANES, mem_width)
    w_sb = w_in_sb.astype(BF16)
    w_kv = w_mem_kv.astype(BF16)
    w_og = w_out_gdn.astype(BF16)
    w_os = w_out_sb.astype(BF16)
    w_u = w_up.astype(BF16)
    w_d = w_down.astype(BF16)

    head_mean = _head_mean_matrix(mem_width, MEM_HEAD_DIM)
    q_gain_t = jnp.tile(mem_q_gain, (1, mem_heads))[:, None, :]
    k_gain_t = jnp.tile(mem_k_gain, (1, mem_heads))[:, None, :]
    lane_pad = lambda a: jnp.pad(a, ((0, 0), (gdn_heads, LANES - n_gate)))[:, None, :]
    neg_exp_alog = lane_pad(-jnp.exp(gdn_a_log))
    dt_bias = lane_pad(gdn_dt_bias)

    pmk, pmv = _mem_kv(mem_prompt, norm_mem[:, None, :], w_kv, k_gain_t, head_mean)
    pmk5 = pmk.reshape(depth, bp, mem_tokens, mem_heads, MEM_HEAD_DIM)
    pmv5 = pmv.reshape(depth, bp, mem_tokens, mem_heads, MEM_HEAD_DIM)
    smk = cache_mem_k.reshape(depth, bs, mem_tokens, mem_width)
    smv = cache_mem_v.reshape(depth, bs, mem_tokens, mem_width)
    cache_k = cache_sb_k.reshape(n_sb, bs, past * sb_heads, SB_HEAD_DIM)
    cache_v = cache_sb_v.reshape(n_sb, bs, past * sb_heads, SB_HEAD_DIM)

    yp, ys = x_prompt, x_sample
    pc, pst, sc, sst = [], [], [], []
    kv_stacks = {True: None, False: None}
    for i in range(depth):
        j = i // 2
        g_mix = norm_mix[i][None, :]
        paths = []
        for y, mk, mv, is_prompt in ((yp, pmk, pmv, True), (ys, smk, smv, False)):
            b, t, _ = y.shape
            y2 = y.reshape(b * t, d_model)
            if i % 2 == 0:
                qkv, z, ba, mq = _in_proj_gdn(y2, g_mix, w_gdn, j, gdn_widths)
                if is_prompt:
                    buf0 = jnp.zeros((b, GDN_CONV - 1, conv_ch), F32)
                    st0 = jnp.zeros((b, gdn_heads, GDN_HEAD_DIM, GDN_HEAD_DIM), F32)
                else:
                    buf0, st0 = state_gdn_conv[j], state_gdn_s[j]
                o_mix, nbuf, s_new = _gdn_mixer(
                    qkv.reshape(b, t, conv_ch), z.reshape(b, t, gdn_width), ba.reshape(b, t, LANES),
                    buf0, st0, gdn_conv_w[j], neg_exp_alog[j], dt_bias[j], gdn_o_gain[j][None, :])
                (pc if is_prompt else sc).append(nbuf)
                (pst if is_prompt else sst).append(s_new)
                w_o = w_og
            else:
                gq, gk = sb_q_gain[j][None, :], sb_k_gain[j][None, :]
                q, kb, vb, mq, k_all, v_all = _in_proj_sb(y2, g_mix, w_sb, gq, gk, sb_heads, mem_width,
                                                          j, n_sb, kv_stacks[is_prompt])
                kv_stacks[is_prompt] = (k_all, v_all)
                r3 = lambda a: a.reshape(b, t, sb_width)
                if is_prompt:
                    tq = min(t, SB_QUERY_BLOCK)
                    o_mix = _sb_attention(r3(q), r3(kb), r3(vb), r3(kb), r3(vb), gq, gk, tq, tq, True)
                else:
                    o_mix = _sb_attention(r3(q), r3(kb), r3(vb), cache_k, cache_v, gq, gk, t,
                                          min(past, SB_CACHE_BLOCK), False, j)
                w_o = w_os
            y = _mix_out(o_mix, mq.reshape(b, t, mem_width), mk, mv, i, q_gain_t, head_mean, w_o, j, y)
            y = _ffn(y.reshape(b * t, d_model), norm_ffn[i][None, :], w_u, w_d, i).reshape(b, t, d_model)
            paths.append(y)
        yp, ys = paths
    heads5 = lambda a, b, t: a.reshape(n_sb, b, t, sb_heads, SB_HEAD_DIM)
    pk, pv = (heads5(a, bp, t_p) for a in kv_stacks[True])
    sk, sv = (heads5(a, bs, t_s) for a in kv_stacks[False])
    return (yp, ys, jnp.stack(pc), jnp.stack(pst), pk, pv, pmk5, pmv5,
            jnp.stack(sc), jnp.stack(sst), sk, sv)
```

```python
import functools
import math

import jax
import jax.numpy as jnp
from jax import lax
from jax.experimental import pallas as pl
from jax.experimental.pallas import tpu as pltpu

F32 = jnp.float32
BF16 = jnp.bfloat16

NORM_EPS = 1e-6
CHUNK = 64
GDN_HEAD_DIM = 128
GDN_CONV = 4
GDN_BLOCK = 256
GDN_GROUP = 128
SB_HEAD_DIM = 128
SB_QUERY_BLOCK = 256
SB_CACHE_BLOCK = 256
MEM_HEAD_DIM = 64
LANES = 128
SUBLANES = 8
VMEM_LIMIT_BYTES = 56 * 1024 * 1024
ROW_TILE = 512
COL_CHUNK = 512
FF_CHUNK = 1024
F32_MIN_NORMAL_LOG = 88.0


def _params(*semantics):
    return pltpu.CompilerParams(dimension_semantics=semantics, vmem_limit_bytes=VMEM_LIMIT_BYTES)


def _dot(a, b):
    return jnp.dot(a.astype(BF16), b.astype(BF16), preferred_element_type=F32)


def _dot_nt(a, b):
    return lax.dot_general(a.astype(BF16), b.astype(BF16), (((1,), (1,)), ((), ())),
                           preferred_element_type=F32)


def _dot_tn(a, b):
    return lax.dot_general(a.astype(BF16), b.astype(BF16), (((0,), (0,)), ((), ())),
                           preferred_element_type=F32)


def _split_bf16(x, terms):
    parts = []
    for _ in range(terms):
        p = x.astype(BF16)
        parts.append(p)
        x = x - p.astype(F32)
    return parts


def _dot_exact_rhs(a, b_exact, terms):
    acc = None
    for p in _split_bf16(a, terms):
        t = jnp.dot(p, b_exact, preferred_element_type=F32)
        acc = t if acc is None else acc + t
    return acc


def _rms(x, gain):
    return x * lax.rsqrt(jnp.mean(x * x, axis=-1, keepdims=True) + NORM_EPS) * gain


def _sigmoid(x):
    return 1.0 / (1.0 + jnp.exp(-x))


def _softplus(x):
    return jnp.maximum(x, 0.0) + jnp.log1p(jnp.exp(-jnp.abs(x)))


def _const_spec(shape):
    zeros = (0,) * len(shape)
    return pl.BlockSpec(shape, lambda *_: zeros)


def _row_spec(tm, n):
    return pl.BlockSpec((tm, n), lambda i: (i, 0))


def _layer_spec(stacked, layer):
    zeros = (0,) * (stacked.ndim - 1)
    return pl.BlockSpec((None,) + stacked.shape[1:], lambda *_: (layer,) + zeros)


def _project(xb, w_ref, o_ref, off, n):
    for c in range(0, n, COL_CHUNK):
        cw = min(COL_CHUNK, n - c)
        o_ref[:, c:c + cw] = jnp.dot(xb, w_ref[:, off + c:off + c + cw],
                                     preferred_element_type=F32).astype(o_ref.dtype)


def _in_proj_gdn_kernel(x_ref, g_ref, w_ref, qkv_ref, z_ref, ba_ref, mq_ref):
    xb = _rms(x_ref[...], g_ref[...]).astype(BF16)
    off = 0
    for o_ref in (qkv_ref, z_ref, ba_ref, mq_ref):
        n = o_ref.shape[-1]
        _project(xb, w_ref, o_ref, off, n)
        off += n


def _in_proj_gdn(x2, gain, w, layer, widths):
    rows, d = x2.shape
    tm = min(ROW_TILE, rows)
    return pl.pallas_call(
        _in_proj_gdn_kernel,
        out_shape=[jax.ShapeDtypeStruct((rows, n), F32) for n in widths],
        grid=(rows // tm,),
        in_specs=[_row_spec(tm, d), _const_spec((1, d)), _layer_spec(w, layer)],
        out_specs=[_row_spec(tm, n) for n in widths],
        compiler_params=_params("parallel"),
        name="in_proj_gdn",
    )(x2, gain, w)


def _in_proj_sb_kernel(x_ref, g_ref, w_ref, gq_ref, gk_ref, k_in_ref, v_in_ref,
                       q_ref, kb_ref, vb_ref, mq_ref, k_ref, v_ref, *, heads):
    del k_in_ref, v_in_ref
    xb = _rms(x_ref[...], g_ref[...]).astype(BF16)
    tm = x_ref.shape[0]
    d = SB_HEAD_DIM
    width = heads * d
    q_all = jnp.dot(xb, w_ref[:, :width], preferred_element_type=F32)
    k_all = jnp.dot(xb, w_ref[:, width:2 * width], preferred_element_type=F32)
    v_all = jnp.dot(xb, w_ref[:, 2 * width:3 * width], preferred_element_type=F32)
    vb_ref[...] = v_all.astype(BF16)
    for h in range(heads):
        cs = slice(h * d, (h + 1) * d)
        interleaved = pl.ds(h, tm, stride=heads)
        q_ref[:, cs] = (_rms(q_all[:, cs], gq_ref[...]) * (d ** -0.5)).astype(BF16)
        k = _rms(k_all[:, cs], gk_ref[...])
        k_ref[interleaved, :] = k
        kb_ref[:, cs] = k.astype(BF16)
        v_ref[interleaved, :] = v_all[:, cs]
    _project(xb, w_ref, mq_ref, 3 * width, mq_ref.shape[-1])


def _in_proj_sb(x2, gain, w, gq, gk, heads, mem_width, slot, kv_all):
    rows, d_model = x2.shape
    d = SB_HEAD_DIM
    width = heads * d
    tm = min(ROW_TILE, rows)
    stack_shape = jax.ShapeDtypeStruct(kv_all[0].shape, F32)
    stack_spec = pl.BlockSpec((None, tm * heads, d), lambda i: (slot, i, 0))
    return pl.pallas_call(
        functools.partial(_in_proj_sb_kernel, heads=heads),
        out_shape=[jax.ShapeDtypeStruct((rows, width), BF16)] * 3
        + [jax.ShapeDtypeStruct((rows, mem_width), F32), stack_shape, stack_shape],
        grid=(rows // tm,),
        in_specs=[_row_spec(tm, d_model), _const_spec((1, d_model)), _layer_spec(w, slot),
                  _const_spec((1, d)), _const_spec((1, d)),
                  pl.BlockSpec(memory_space=pl.ANY), pl.BlockSpec(memory_space=pl.ANY)],
        out_specs=[_row_spec(tm, width)] * 3 + [_row_spec(tm, mem_width), stack_spec, stack_spec],
        input_output_aliases={5: 4, 6: 5},
        compiler_params=_params("parallel"),
        name="in_proj_sb",
    )(x2, gain, w, gq, gk, *kv_all)


def _head_mean_sq(x, hm_ref):
    return _dot_exact_rhs(x * x, hm_ref[...], 3)


def _mem_kv_kernel(mem_ref, g_ref, w_ref, kg_ref, hm_ref, k_ref, v_ref):
    xb = _rms(mem_ref[...], g_ref[...]).astype(BF16)
    width = k_ref.shape[-1]
    k = jnp.dot(xb, w_ref[:, :width], preferred_element_type=F32)
    k_ref[...] = k * lax.rsqrt(_head_mean_sq(k, hm_ref) + NORM_EPS) * kg_ref[...]
    v_ref[...] = jnp.dot(xb, w_ref[:, width:], preferred_element_type=F32)


def _mem_kv(mem, norm_mem, w_kv, k_gain_t, head_mean):
    bp, m, d = mem.shape
    depth = w_kv.shape[0]
    width = w_kv.shape[-1] // 2
    return pl.pallas_call(
        _mem_kv_kernel,
        out_shape=[jax.ShapeDtypeStruct((depth, bp, m, width), F32)] * 2,
        grid=(depth, bp),
        in_specs=[pl.BlockSpec((None, m, d), lambda i, b: (b, 0, 0)),
                  pl.BlockSpec((None, 1, d), lambda i, b: (i, 0, 0)),
                  pl.BlockSpec((None, d, 2 * width), lambda i, b: (i, 0, 0)),
                  pl.BlockSpec((None, 1, width), lambda i, b: (i, 0, 0)),
                  _const_spec((width, width))],
        out_specs=[pl.BlockSpec((None, None, m, width), lambda i, b: (i, b, 0, 0))] * 2,
        compiler_params=_params("parallel", "parallel"),
        name="mem_kv",
    )(mem, norm_mem, w_kv, k_gain_t, head_mean)


def _mix_out_kernel(o_ref, mq_ref, k_ref, v_ref, qg_ref, hm_ref, w_ref, x_ref, y_ref):
    q = mq_ref[...]
    mw = q.shape[-1]
    qn = q * lax.rsqrt(_head_mean_sq(q, hm_ref) + NORM_EPS) * qg_ref[...]
    kb = k_ref[...].astype(BF16)
    v = v_ref[...]
    lane_head = lax.broadcasted_iota(jnp.int32, (1, mw), 1) // MEM_HEAD_DIM
    scores = [_dot_nt(jnp.where(lane_head == h, qn, 0.0), kb) * (MEM_HEAD_DIM ** -0.5)
              for h in range(mw // MEM_HEAD_DIM)]
    mem = None
    for h, sh in enumerate(scores):
        e = jnp.exp(sh - jnp.max(sh, axis=-1, keepdims=True))
        p = e / jnp.sum(e, axis=-1, keepdims=True)
        part = _dot(p, jnp.where(lane_head == h, v, 0.0))
        mem = part if mem is None else mem + part
    w_mix = o_ref.shape[-1]
    y_ref[...] = x_ref[...] + _dot(o_ref[...], w_ref[:w_mix, :]) + _dot(mem, w_ref[w_mix:, :])


def _mix_out(o_mix, mq, mk, mv, layer, q_gain_t, head_mean, w_out, w_layer, x):
    b, t, d = x.shape
    w_mix = o_mix.shape[-1]
    mw = mq.shape[-1]
    tm = min(ROW_TILE, t)
    tok = lambda n: pl.BlockSpec((None, tm, n), lambda bi, i: (bi, i, 0))
    kv = pl.BlockSpec((None, None) + mk.shape[2:], lambda bi, i: (layer, bi, 0, 0))
    return pl.pallas_call(
        _mix_out_kernel,
        out_shape=jax.ShapeDtypeStruct((b, t, d), F32),
        grid=(b, t // tm),
        in_specs=[tok(w_mix), tok(mw), kv, kv, _layer_spec(q_gain_t, layer), _const_spec((mw, mw)),
                  _layer_spec(w_out, w_layer), tok(d)],
        out_specs=tok(d),
        compiler_params=_params("parallel", "parallel"),
        name="mix_out",
    )(o_mix, mq, mk, mv, q_gain_t, head_mean, w_out, x)


def _ffn_kernel(x_ref, g_ref, wu_ref, wd_ref, y_ref):
    x = x_ref[...]
    hb = _rms(x, g_ref[...]).astype(BF16)
    acc = x
    for c in range(0, wu_ref.shape[-1], FF_CHUNK):
        a = jnp.maximum(jnp.dot(hb, wu_ref[:, c:c + FF_CHUNK], preferred_element_type=F32), 0.0)
        acc = acc + jnp.dot((a * a).astype(BF16), wd_ref[c:c + FF_CHUNK, :], preferred_element_type=F32)
    y_ref[...] = acc


def _ffn(x2, gain, w_up, w_down, layer):
    rows, d = x2.shape
    tm = min(ROW_TILE, rows)
    return pl.pallas_call(
        _ffn_kernel,
        out_shape=jax.ShapeDtypeStruct((rows, d), F32),
        grid=(rows // tm,),
        in_specs=[_row_spec(tm, d), _const_spec((1, d)), _layer_spec(w_up, layer), _layer_spec(w_down, layer)],
        out_specs=_row_spec(tm, d),
        compiler_params=_params("parallel"),
        name="ffn",
    )(x2, gain, w_up, w_down)


CONV_PAD = SUBLANES


def _cat3(parts):
    hi, lo = parts
    return jnp.concatenate([hi, lo, hi], axis=1)


def _unit_lower_inverses(lowers, eye):
    r = eye.shape[0]
    ps = [-lower for lower in lowers]
    invs = [eye + p for p in ps]

    def split_rhs(p):
        parts = _split_bf16(p, 2)
        return parts, jnp.concatenate([parts[0], parts[0], parts[1]], axis=0)

    ps = [jnp.dot(_cat3(parts), rhs, preferred_element_type=F32) for parts, rhs in map(split_rhs, ps)]
    span = 4
    while span <= CHUNK:
        nxt_p, nxt_inv = [], []
        for p, inv in zip(ps, invs):
            parts, rhs = split_rhs(p)
            i_cat = _cat3(_split_bf16(inv, 2))
            if span < CHUNK:
                both = jnp.dot(jnp.concatenate([_cat3(parts), i_cat], axis=0), rhs, preferred_element_type=F32)
                nxt_p.append(both[:r])
                nxt_inv.append(inv + both[r:])
            else:
                nxt_inv.append(inv + jnp.dot(i_cat, rhs, preferred_element_type=F32))
        ps, invs = nxt_p, nxt_inv
        span *= 2

    def cat6_lhs(x):
        a0, a1, a2 = _split_bf16(x, 3)
        return jnp.concatenate([a0, a0, a0, a1, a1, a2], axis=1)

    def cat6_rhs(x):
        b0, b1, b2 = _split_bf16(x, 3)
        return jnp.concatenate([b0, b1, b2, b0, b1, b0], axis=0)

    resids = [eye - inv - jnp.dot(cat6_lhs(lower), cat6_rhs(inv), preferred_element_type=F32)
              for lower, inv in zip(lowers, invs)]
    return [inv + _dot(inv, resid) for inv, resid in zip(invs, resids)]


def _gdn_kernel(qkv_ref, z_ref, ba_ref, buf_ref, s0_ref, cw_ref, nal_ref, dt_ref, og_ref,
                o_ref, nbuf_ref, sout_ref,
                xp_ref, act_ref, s_ref, gc_ref, kwqw_ref, ku_ref, au_ref, *, heads):
    blk = pl.program_id(1)
    tb = qkv_ref.shape[0]
    hd = GDN_HEAD_DIM
    width = heads * hd
    tail = GDN_CONV - 1
    group = min(tb, GDN_GROUP)
    n_sub = group // CHUNK

    @pl.when(blk == 0)
    def _():
        xp_ref[CONV_PAD - tail:CONV_PAD, :] = buf_ref[...]
        s_ref[...] = s0_ref[...]

    xp_ref[CONV_PAD:CONV_PAD + tb, :] = qkv_ref[...]

    for ct in range(3 * heads):
        cs = slice(ct * hd, (ct + 1) * hd)
        y = None
        for j in range(GDN_CONV):
            r0 = CONV_PAD - tail + j
            term = xp_ref[r0:r0 + tb, cs] * cw_ref[j:j + 1, cs]
            y = term if y is None else y + term
        y = y * _sigmoid(y)
        if ct < 2 * heads:
            y = y * lax.rsqrt(jnp.sum(y * y, axis=-1, keepdims=True) + NORM_EPS)
            if ct < heads:
                y = y * (hd ** -0.5)
        act_ref[:, cs] = y

    new_tail = xp_ref[CONV_PAD + tb - tail:CONV_PAD + tb, :]
    xp_ref[CONV_PAD - tail:CONV_PAD, :] = new_tail

    ba = ba_ref[...]
    beta_all = _sigmoid(ba)
    g_all = nal_ref[...] * _softplus(ba + dt_ref[...])

    row = lax.broadcasted_iota(jnp.int32, (group, group), 0)
    col = lax.broadcasted_iota(jnp.int32, (group, group), 1)
    same = (row // CHUNK) == (col // CHUNK)
    causal = jnp.logical_and(same, row >= col)
    strict = jnp.logical_and(same, row > col)
    tri3 = jnp.concatenate([causal.astype(BF16)] * 3, axis=1)
    upper_f = jnp.logical_and(same, row <= col).astype(F32)
    eye = (row == col).astype(F32)
    sub_of_row = lax.broadcasted_iota(jnp.int32, (group, hd), 0) // CHUNK

    pairs = [(gi, h) for gi in range(tb // group) for h in range(heads)]
    gcs = []
    for gi in range(tb // group):
        rs = slice(gi * group, (gi + 1) * group)
        gc = jnp.dot(tri3, jnp.concatenate(_split_bf16(g_all[rs, :], 3), axis=0), preferred_element_type=F32)
        gc_ref[rs, :] = gc
        gcs.append(gc)

    def head_slices(gi, h):
        rs = slice(gi * group, (gi + 1) * group)
        return (act_ref[rs, h * hd:(h + 1) * hd], act_ref[rs, width + h * hd:width + (h + 1) * hd],
                act_ref[rs, 2 * width + h * hd:2 * width + (h + 1) * hd])

    def gate_columns(gi, h):
        gcol = gcs[gi][:, heads + h:heads + h + 1]
        bcol = beta_all[gi * group:(gi + 1) * group, h:h + 1]
        return gcol, bcol

    lowers, a_intras = [], []
    for gi, h in pairs:
        rs = slice(gi * group, (gi + 1) * group)
        gcol, bcol = gate_columns(gi, h)
        grow = jnp.sum(g_all[rs, heads + h:heads + h + 1] * upper_f, axis=0, keepdims=True)
        decay = jnp.where(causal, jnp.exp(jnp.where(causal, gcol - grow, 0.0)), 0.0)
        q, k, _ = head_slices(gi, h)
        kq = _dot_nt(jnp.concatenate([k * bcol, q], axis=0), k)
        lowers.append(kq[:group] * jnp.where(strict, decay, 0.0))
        a_intras.append(kq[group:] * decay)

    invs = _unit_lower_inverses(lowers, eye)

    sols = []
    for (gi, h), inv in zip(pairs, invs):
        gcol, bcol = gate_columns(gi, h)
        _, k, v = head_slices(gi, h)
        sols.append(_dot(inv, jnp.concatenate([v * bcol, k * bcol * jnp.exp(gcol)], axis=-1)).astype(BF16))

    for (gi, h), sol, a_intra in zip(pairs, sols, a_intras):
        gcol, _ = gate_columns(gi, h)
        glast = jnp.concatenate(
            [jnp.broadcast_to(gcol[(s + 1) * CHUNK - 1:(s + 1) * CHUNK, :], (CHUNK, 1)) for s in range(n_sub)],
            axis=0)
        q, k, _ = head_slices(gi, h)
        k_dec = k * jnp.exp(glast - gcol)
        if n_sub > 1:
            k_dec = jnp.concatenate([jnp.where(sub_of_row == s, k_dec, 0.0) for s in range(n_sub)], axis=1)
        kd_sol = _dot_tn(k_dec, sol)
        a_sol = _dot(a_intra, sol)
        qw = q * jnp.exp(gcol) - a_sol[:, hd:]
        for s in range(n_sub):
            idx = (gi * n_sub + s) * heads + h
            ku_ref[idx] = kd_sol[s * hd:(s + 1) * hd, :hd]
            au_ref[idx] = a_sol[s * CHUNK:(s + 1) * CHUNK, :hd]
            kwqw_ref[idx, :hd, :] = kd_sol[s * hd:(s + 1) * hd, hd:].astype(BF16)
            kwqw_ref[idx, hd:, :] = qw[s * CHUNK:(s + 1) * CHUNK, :].astype(BF16)

    for c in range(tb // CHUNK):
        rs = slice(c * CHUNK, (c + 1) * CHUNK)
        for h in range(heads):
            idx = c * heads + h
            gl = heads + h
            s = s_ref[h]
            glast = gc_ref[(c + 1) * CHUNK - 1:(c + 1) * CHUNK, gl:gl + 1]
            r = jnp.dot(kwqw_ref[idx], s.astype(BF16), preferred_element_type=F32)
            s_ref[h] = s * jnp.exp(glast) - r[:hd] + ku_ref[idx]
            o = r[hd:] + au_ref[idx]
            zh = z_ref[rs, h * hd:(h + 1) * hd]
            o_ref[rs, h * hd:(h + 1) * hd] = _rms(o, og_ref[...]) * (zh * _sigmoid(zh))

    @pl.when(blk == pl.num_programs(1) - 1)
    def _():
        nbuf_ref[...] = new_tail
        sout_ref[...] = s_ref[...]


def _gdn_mixer(qkv, z, ba, conv_buf, s0, conv_w, neg_exp_alog, dt_bias, o_gain):
    b, t, cw = qkv.shape
    heads = s0.shape[1]
    hd = GDN_HEAD_DIM
    width = heads * hd
    tail = GDN_CONV - 1
    tb = min(t, GDN_BLOCK)
    n_ch = (tb // CHUNK) * heads
    tok = lambda n: pl.BlockSpec((None, tb, n), lambda bi, i: (bi, i, 0))
    return pl.pallas_call(
        functools.partial(_gdn_kernel, heads=heads),
        out_shape=[jax.ShapeDtypeStruct((b, t, width), F32),
                   jax.ShapeDtypeStruct((b, tail, cw), F32),
                   jax.ShapeDtypeStruct((b, heads, hd, hd), F32)],
        grid=(b, t // tb),
        in_specs=[tok(cw), tok(width), tok(LANES),
                  pl.BlockSpec((None, tail, cw), lambda bi, i: (bi, 0, 0)),
                  pl.BlockSpec((None, heads, hd, hd), lambda bi, i: (bi, 0, 0, 0)),
                  _const_spec((GDN_CONV, cw)), _const_spec((1, LANES)), _const_spec((1, LANES)),
                  _const_spec((1, hd))],
        out_specs=[tok(width),
                   pl.BlockSpec((None, tail, cw), lambda bi, i: (bi, 0, 0)),
                   pl.BlockSpec((None, heads, hd, hd), lambda bi, i: (bi, 0, 0, 0))],
        scratch_shapes=[pltpu.VMEM((CONV_PAD + tb, cw), F32), pltpu.VMEM((tb, cw), F32),
                        pltpu.VMEM((heads, hd, hd), F32), pltpu.VMEM((tb, LANES), F32),
                        pltpu.VMEM((n_ch, hd + CHUNK, hd), BF16), pltpu.VMEM((n_ch, hd, hd), F32),
                        pltpu.VMEM((n_ch, CHUNK, hd), F32)],
        compiler_params=_params("parallel", "arbitrary"),
        name="gdn_mixer",
    )(qkv, z, ba, conv_buf, s0, conv_w, neg_exp_alog, dt_bias, o_gain)


def _suffix_sums(m, tri2):
    return jnp.dot(jnp.concatenate(_split_bf16(m, 2), axis=1), tri2, preferred_element_type=F32)


def _tri2(n):
    row = lax.broadcasted_iota(jnp.int32, (n, n), 0)
    col = lax.broadcasted_iota(jnp.int32, (n, n), 1)
    tri = (row >= col).astype(BF16)
    return jnp.concatenate([tri, tri], axis=0), row > col


def _sb_attn_kernel(q_ref, kd_ref, vd_ref, *refs, heads, tk, off_blocks, interleaved, has_prev):
    if has_prev:
        kp_ref, vp_ref = refs[:2]
        refs = refs[2:]
    ko_ref, vo_ref, gq_ref, gk_ref, o_ref, acc_ref, suf_ref = refs
    tq = q_ref.shape[0]
    d = SB_HEAD_DIM
    n_off = off_blocks(pl.program_id(1))

    tri_q, strict = _tri2(tq)
    tri_k = tri_q if tk == tq else _tri2(tk)[0]
    cols = [slice(h * d, (h + 1) * d) for h in range(heads)]
    zs = [_dot_nt(q_ref[:, cs], kd_ref[:, cs]) for cs in cols]
    cds = [_suffix_sums(jnp.where(strict, jnp.log1p(jnp.exp(z)), 0.0), tri_q) for z in zs]
    sufs = [c[:, 0:1] for c in cds]
    weights = [jnp.where(strict, jnp.exp(z - c), 0.0) for z, c in zip(zs, cds)]
    if has_prev:
        valid = n_off > 0
        zps = [_dot_nt(q_ref[:, cs], kp_ref[:, cs]) for cs in cols]
        cps = [_suffix_sums(jnp.log1p(jnp.exp(zp)), tri_k) for zp in zps]
        weights_p = [jnp.where(valid, jnp.exp(zp - cp - suf), 0.0) for zp, cp, suf in zip(zps, cps, sufs)]
        sufs = [suf + jnp.where(valid, cp[:, 0:1], 0.0) for suf, cp in zip(sufs, cps)]
    suf_min = None
    for h, cs in enumerate(cols):
        acc = _dot(weights[h], vd_ref[:, cs])
        if has_prev:
            acc = acc + _dot(weights_p[h], vp_ref[:, cs])
        acc_ref[:, cs] = acc
        suf_ref[h] = sufs[h]
        hmin = jnp.min(sufs[h])
        suf_min = hmin if suf_min is None else jnp.minimum(suf_min, hmin)

    z_bound = (jnp.max(jnp.abs(gq_ref[...])) * jnp.max(jnp.abs(gk_ref[...]))
               * (math.sqrt(SB_HEAD_DIM) * 1.01))

    def live(carry):
        n, smin = carry
        return jnp.logical_and(n >= 0, z_bound - smin > -F32_MIN_NORMAL_LOG)

    def step(carry):
        n, _ = carry
        kvs = []
        for h, cs in enumerate(cols):
            if interleaved:
                rows = pl.ds(n * (tk * heads) + h, tk, stride=heads)
                kvs.append((ko_ref[rows, :], vo_ref[rows, :]))
            else:
                rows = pl.ds(pl.multiple_of(n * tk, tk), tk)
                kvs.append((ko_ref[rows, cs], vo_ref[rows, cs]))
        zos = [_dot_nt(q_ref[:, cs], kb) for cs, (kb, _) in zip(cols, kvs)]
        cos = [_suffix_sums(jnp.log1p(jnp.exp(zo)), tri_k) for zo in zos]
        smin = None
        for h, cs in enumerate(cols):
            suf = suf_ref[h]
            acc_ref[:, cs] += _dot(jnp.exp(zos[h] - cos[h] - suf), kvs[h][1])
            suf = suf + cos[h][:, 0:1]
            suf_ref[h] = suf
            hmin = jnp.min(suf)
            smin = hmin if smin is None else jnp.minimum(smin, hmin)
        return n - 1, smin

    lax.while_loop(live, step, (n_off - (2 if has_prev else 1), suf_min))
    o_ref[...] = acc_ref[...]


def _sb_attention(q, k_new, v_new, k_off, v_off, gq, gk, tq, tk, causal_prompt, layer=0):
    b, t, width = q.shape
    d = SB_HEAD_DIM
    heads = width // d
    blk = pl.BlockSpec((None, tq, width), lambda bi, i: (bi, i, 0))
    if causal_prompt:
        full = pl.BlockSpec((None,) + k_off.shape[1:], lambda bi, i: (bi, 0, 0), pipeline_mode=pl.Buffered(1))
    else:
        full = pl.BlockSpec((None, None) + k_off.shape[2:], lambda bi, i: (layer, bi, 0, 0),
                            pipeline_mode=pl.Buffered(1))
    in_specs = [blk, blk, blk]
    args = [q, k_new, v_new]
    if causal_prompt:
        assert tq == tk
        off_blocks = lambda i: i
        prev = pl.BlockSpec((None, tk, width), lambda bi, i: (bi, jnp.maximum(i - 1, 0), 0))
        in_specs += [prev, prev]
        args += [k_off, v_off]
    else:
        t_off = k_off.shape[2] // heads
        off_blocks = lambda i: t_off // tk
    in_specs += [full, full, _const_spec((1, d)), _const_spec((1, d))]
    args += [k_off, v_off, gq, gk]
    return pl.pallas_call(
        functools.partial(_sb_attn_kernel, heads=heads, tk=tk, off_blocks=off_blocks,
                          interleaved=not causal_prompt, has_prev=causal_prompt),
        out_shape=jax.ShapeDtypeStruct((b, t, width), F32),
        grid=(b, t // tq),
        in_specs=in_specs,
        out_specs=blk,
        scratch_shapes=[pltpu.VMEM((tq, width), F32), pltpu.VMEM((heads, tq, 1), F32)],
        compiler_params=_params("parallel", "arbitrary"),
        name="sb_attention",
    )(*args)


def _head_mean_matrix(width, head_dim):
    idx = jnp.arange(width) // head_dim
    return ((idx[:, None] == idx[None, :]).astype(F32) / head_dim).astype(BF16)


def kernel(x_prompt, x_sample, state_gdn_conv, state_gdn_s, cache_sb_k, cache_sb_v, cache_mem_k, cache_mem_v,
           mem_prompt, norm_mix, norm_mem, norm_ffn, w_in_gdn, w_in_sb, w_mem_kv, mem_q_gain, mem_k_gain,
           gdn_conv_w, gdn_a_log, gdn_dt_bias, gdn_o_gain, sb_q_gain, sb_k_gain, w_out_gdn, w_out_sb, w_up, w_down):
    depth, d_model = norm_mix.shape
    bp, t_p, _ = x_prompt.shape
    bs, t_s, _ = x_sample.shape
    gdn_heads = gdn_a_log.shape[1]
    gdn_width = gdn_heads * GDN_HEAD_DIM
    conv_ch = gdn_conv_w.shape[-1]
    n_sb, _, past, sb_heads, _ = cache_sb_k.shape
    sb_width = sb_heads * SB_HEAD_DIM
    mem_tokens, mem_heads = cache_mem_k.shape[2], cache_mem_k.shape[3]
    mem_width = mem_heads * MEM_HEAD_DIM
    assert MEM_HEAD_DIM & (MEM_HEAD_DIM - 1) == 0

    n_gate = 2 * gdn_heads
    tok_in = conv_ch + gdn_width + n_gate
    w_g = w_in_gdn
    w_gdn = jnp.concatenate(
        [w_g[..., :conv_ch + gdn_width],
         jnp.pad(w_g[..., conv_ch + gdn_width:tok_in], ((0, 0), (0, 0), (0, LANES - n_gate))),
         w_g[..., tok_in:]], axis=-1).astype(BF16)
    gdn_widths = (conv_ch, gdn_width, LANES, mem_width)
    w_sb = w_in_sb.astype(BF16)
    w_kv = w_mem_kv.astype(BF16)
    w_og = w_out_gdn.astype(BF16)
    w_os = w_out_sb.astype(BF16)
    w_u = w_up.astype(BF16)
    w_d = w_down.astype(BF16)

    head_mean = _head_mean_matrix(mem_width, MEM_HEAD_DIM)
    q_gain_t = jnp.tile(mem_q_gain, (1, mem_heads))[:, None, :]
    k_gain_t = jnp.tile(mem_k_gain, (1, mem_heads))[:, None, :]
    lane_pad = lambda a: jnp.pad(a, ((0, 0), (gdn_heads, LANES - n_gate)))[:, None, :]
    neg_exp_alog = lane_pad(-jnp.exp(gdn_a_log))
    dt_bias = lane_pad(gdn_dt_bias)

    pmk, pmv = _mem_kv(mem_prompt, norm_mem[:, None, :], w_kv, k_gain_t, head_mean)
    pmk5 = pmk.reshape(depth, bp, mem_tokens, mem_heads, MEM_HEAD_DIM)
    pmv5 = pmv.reshape(depth, bp, mem_tokens, mem_heads, MEM_HEAD_DIM)
    smk = cache_mem_k.reshape(depth, bs, mem_tokens, mem_width)
    smv = cache_mem_v.reshape(depth, bs, mem_tokens, mem_width)
    cache_k = cache_sb_k.reshape(n_sb, bs, past * sb_heads, SB_HEAD_DIM)
    cache_v = cache_sb_v.reshape(n_sb, bs, past * sb_heads, SB_HEAD_DIM)

    yp, ys = x_prompt, x_sample
    pc, pst, sc, sst = [], [], [], []
    kv_stacks = {prompt: tuple(jnp.zeros((n_sb, rows * sb_heads, SB_HEAD_DIM), F32) for _ in range(2))
                 for prompt, rows in ((True, bp * t_p), (False, bs * t_s))}
    for i in range(depth):
        j = i // 2
        g_mix = norm_mix[i][None, :]
        paths = []
        for y, mk, mv, is_prompt in ((yp, pmk, pmv, True), (ys, smk, smv, False)):
            b, t, _ = y.shape
            y2 = y.reshape(b * t, d_model)
            if i % 2 == 0:
                qkv, z, ba, mq = _in_proj_gdn(y2, g_mix, w_gdn, j, gdn_widths)
                if is_prompt:
                    buf0 = jnp.zeros((b, GDN_CONV - 1, conv_ch), F32)
                    st0 = jnp.zeros((b, gdn_heads, GDN_HEAD_DIM, GDN_HEAD_DIM), F32)
                else:
                    buf0, st0 = state_gdn_conv[j], state_gdn_s[j]
                o_mix, nbuf, s_new = _gdn_mixer(
                    qkv.reshape(b, t, conv_ch), z.reshape(b, t, gdn_width), ba.reshape(b, t, LANES),
                    buf0, st0, gdn_conv_w[j], neg_exp_alog[j], dt_bias[j], gdn_o_gain[j][None, :])
                (pc if is_prompt else sc).append(nbuf)
                (pst if is_prompt else sst).append(s_new)
                w_o = w_og
            else:
                gq, gk = sb_q_gain[j][None, :], sb_k_gain[j][None, :]
                q, kb, vb, mq, k_all, v_all = _in_proj_sb(y2, g_mix, w_sb, gq, gk, sb_heads, mem_width,
                                                          j, kv_stacks[is_prompt])
                kv_stacks[is_prompt] = (k_all, v_all)
                r3 = lambda a: a.reshape(b, t, sb_width)
                if is_prompt:
                    tq = min(t, SB_QUERY_BLOCK)
                    o_mix = _sb_attention(r3(q), r3(kb), r3(vb), r3(kb), r3(vb), gq, gk, tq, tq, True)
                else:
                    o_mix = _sb_attention(r3(q), r3(kb), r3(vb), cache_k, cache_v, gq, gk, t,
                                          min(past, SB_CACHE_BLOCK), False, j)
                w_o = w_os
            y = _mix_out(o_mix, mq.reshape(b, t, mem_width), mk, mv, i, q_gain_t, head_mean, w_o, j, y)
            y = _ffn(y.reshape(b * t, d_model), norm_ffn[i][None, :], w_u, w_d, i).reshape(b, t, d_model)
            paths.append(y)
        yp, ys = paths
    heads5 = lambda a, b, t: a.reshape(n_sb, b, t, sb_heads, SB_HEAD_DIM)
    pk, pv = (heads5(a, bp, t_p) for a in kv_stacks[True])
    sk, sv = (heads5(a, bs, t_s) for a in kv_stacks[False])
    return (yp, ys, jnp.stack(pc), jnp.stack(pst), pk, pv, pmk5, pmv5,
            jnp.stack(sc), jnp.stack(sst), sk, sv)
```

```python
import functools
import math

import jax
import jax.numpy as jnp
from jax import lax
from jax.experimental import pallas as pl
from jax.experimental.pallas import tpu as pltpu

F32 = jnp.float32
BF16 = jnp.bfloat16

NORM_EPS = 1e-6
CHUNK = 64
GDN_HEAD_DIM = 128
GDN_CONV = 4
GDN_BLOCK = 256
GDN_GROUP = 128
SB_HEAD_DIM = 128
SB_QUERY_BLOCK = 256
SB_CACHE_BLOCK = 256
MEM_HEAD_DIM = 64
LANES = 128
SUBLANES = 8
VMEM_LIMIT_BYTES = 56 * 1024 * 1024
ROW_TILE = 512
COL_CHUNK = 512
FF_CHUNK = 1024
F32_MIN_NORMAL_LOG = 88.0


def _params(*semantics):
    return pltpu.CompilerParams(dimension_semantics=semantics, vmem_limit_bytes=VMEM_LIMIT_BYTES)


def _dot(a, b):
    return jnp.dot(a.astype(BF16), b.astype(BF16), preferred_element_type=F32)


def _dot_nt(a, b):
    return lax.dot_general(a.astype(BF16), b.astype(BF16), (((1,), (1,)), ((), ())),
                           preferred_element_type=F32)


def _dot_tn(a, b):
    return lax.dot_general(a.astype(BF16), b.astype(BF16), (((0,), (0,)), ((), ())),
                           preferred_element_type=F32)


def _split_bf16(x, terms):
    parts = []
    for _ in range(terms):
        p = x.astype(BF16)
        parts.append(p)
        x = x - p.astype(F32)
    return parts


def _dot_exact_rhs(a, b_exact, terms):
    acc = None
    for p in _split_bf16(a, terms):
        t = jnp.dot(p, b_exact, preferred_element_type=F32)
        acc = t if acc is None else acc + t
    return acc


def _rms(x, gain):
    return x * lax.rsqrt(jnp.mean(x * x, axis=-1, keepdims=True) + NORM_EPS) * gain


def _sigmoid(x):
    return 1.0 / (1.0 + jnp.exp(-x))


def _softplus(x):
    return jnp.maximum(x, 0.0) + jnp.log1p(jnp.exp(-jnp.abs(x)))


def _const_spec(shape):
    zeros = (0,) * len(shape)
    return pl.BlockSpec(shape, lambda *_: zeros)


def _row_spec(tm, n):
    return pl.BlockSpec((tm, n), lambda i: (i, 0))


def _layer_spec(stacked, layer):
    zeros = (0,) * (stacked.ndim - 1)
    return pl.BlockSpec((None,) + stacked.shape[1:], lambda *_: (layer,) + zeros)


def _project(xb, w_ref, o_ref, off, n):
    for c in range(0, n, COL_CHUNK):
        cw = min(COL_CHUNK, n - c)
        o_ref[:, c:c + cw] = jnp.dot(xb, w_ref[:, off + c:off + c + cw],
                                     preferred_element_type=F32).astype(o_ref.dtype)


def _in_proj_sb_kernel(x_ref, g_ref, w_ref, gq_ref, gk_ref, k_in_ref, v_in_ref,
                       q_ref, kb_ref, vb_ref, mq_ref, k_ref, v_ref, *, heads):
    del k_in_ref, v_in_ref
    xb = _rms(x_ref[...], g_ref[...]).astype(BF16)
    tm = x_ref.shape[0]
    d = SB_HEAD_DIM
    width = heads * d
    q_all = jnp.dot(xb, w_ref[:, :width], preferred_element_type=F32)
    k_all = jnp.dot(xb, w_ref[:, width:2 * width], preferred_element_type=F32)
    v_all = jnp.dot(xb, w_ref[:, 2 * width:3 * width], preferred_element_type=F32)
    vb_ref[...] = v_all.astype(BF16)
    for h in range(heads):
        cs = slice(h * d, (h + 1) * d)
        interleaved = pl.ds(h, tm, stride=heads)
        q_ref[:, cs] = (_rms(q_all[:, cs], gq_ref[...]) * (d ** -0.5)).astype(BF16)
        k = _rms(k_all[:, cs], gk_ref[...])
        k_ref[interleaved, :] = k
        kb_ref[:, cs] = k.astype(BF16)
        v_ref[interleaved, :] = v_all[:, cs]
    _project(xb, w_ref, mq_ref, 3 * width, mq_ref.shape[-1])


def _in_proj_sb(x2, gain, w, gq, gk, heads, mem_width, slot, kv_all):
    rows, d_model = x2.shape
    d = SB_HEAD_DIM
    width = heads * d
    tm = min(ROW_TILE, rows)
    stack_shape = jax.ShapeDtypeStruct(kv_all[0].shape, F32)
    stack_spec = pl.BlockSpec((None, tm * heads, d), lambda i: (slot, i, 0))
    return pl.pallas_call(
        functools.partial(_in_proj_sb_kernel, heads=heads),
        out_shape=[jax.ShapeDtypeStruct((rows, width), BF16)] * 3
        + [jax.ShapeDtypeStruct((rows, mem_width), F32), stack_shape, stack_shape],
        grid=(rows // tm,),
        in_specs=[_row_spec(tm, d_model), _const_spec((1, d_model)), _layer_spec(w, slot),
                  _const_spec((1, d)), _const_spec((1, d)),
                  pl.BlockSpec(memory_space=pl.ANY), pl.BlockSpec(memory_space=pl.ANY)],
        out_specs=[_row_spec(tm, width)] * 3 + [_row_spec(tm, mem_width), stack_spec, stack_spec],
        input_output_aliases={5: 4, 6: 5},
        compiler_params=_params("parallel"),
        name="in_proj_sb",
    )(x2, gain, w, gq, gk, *kv_all)


def _head_mean_sq(x, hm_ref):
    return _dot_exact_rhs(x * x, hm_ref[...], 2)


def _mem_kv_kernel(mem_ref, g_ref, w_ref, kg_ref, hm_ref, k_ref, v_ref):
    xb = _rms(mem_ref[...], g_ref[...]).astype(BF16)
    width = k_ref.shape[-1]
    k = jnp.dot(xb, w_ref[:, :width], preferred_element_type=F32)
    k_ref[...] = k * lax.rsqrt(_head_mean_sq(k, hm_ref) + NORM_EPS) * kg_ref[...]
    v_ref[...] = jnp.dot(xb, w_ref[:, width:], preferred_element_type=F32)


def _mem_kv(mem, norm_mem, w_kv, k_gain_t, head_mean):
    bp, m, d = mem.shape
    depth = w_kv.shape[0]
    width = w_kv.shape[-1] // 2
    return pl.pallas_call(
        _mem_kv_kernel,
        out_shape=[jax.ShapeDtypeStruct((depth, bp, m, width), F32)] * 2,
        grid=(depth, bp),
        in_specs=[pl.BlockSpec((None, m, d), lambda i, b: (b, 0, 0)),
                  pl.BlockSpec((None, 1, d), lambda i, b: (i, 0, 0)),
                  pl.BlockSpec((None, d, 2 * width), lambda i, b: (i, 0, 0)),
                  pl.BlockSpec((None, 1, width), lambda i, b: (i, 0, 0)),
                  _const_spec((width, width))],
        out_specs=[pl.BlockSpec((None, None, m, width), lambda i, b: (i, b, 0, 0))] * 2,
        compiler_params=_params("parallel", "parallel"),
        name="mem_kv",
    )(mem, norm_mem, w_kv, k_gain_t, head_mean)


def _mix_ffn_kernel(o_ref, mq_ref, k_ref, v_ref, qg_ref, hm_ref, wo_ref, x_ref, gf_ref, wu_ref, wd_ref,
                    y_ref, ymid_ref):
    q = mq_ref[...]
    mw = q.shape[-1]
    qn = q * lax.rsqrt(_head_mean_sq(q, hm_ref) + NORM_EPS) * qg_ref[...]
    kb = k_ref[...].astype(BF16)
    v = v_ref[...]
    lane_head = lax.broadcasted_iota(jnp.int32, (1, mw), 1) // MEM_HEAD_DIM
    scores = [_dot_nt(jnp.where(lane_head == h, qn, 0.0), kb) * (MEM_HEAD_DIM ** -0.5)
              for h in range(mw // MEM_HEAD_DIM)]
    mem = None
    for h, sh in enumerate(scores):
        e = jnp.exp(sh - jnp.max(sh, axis=-1, keepdims=True))
        p = e / jnp.sum(e, axis=-1, keepdims=True)
        part = _dot(p, jnp.where(lane_head == h, v, 0.0))
        mem = part if mem is None else mem + part
    w_mix = o_ref.shape[-1]
    ymid_ref[...] = x_ref[...] + _dot(o_ref[...], wo_ref[:w_mix, :]) + _dot(mem, wo_ref[w_mix:, :])

    hb = _rms(ymid_ref[...], gf_ref[...]).astype(BF16)
    acc = None
    for c in range(0, wu_ref.shape[-1], FF_CHUNK):
        a = jnp.maximum(jnp.dot(hb, wu_ref[:, c:c + FF_CHUNK], preferred_element_type=F32), 0.0)
        t = jnp.dot((a * a).astype(BF16), wd_ref[c:c + FF_CHUNK, :], preferred_element_type=F32)
        acc = t if acc is None else acc + t
    y_ref[...] = ymid_ref[...] + acc


def _resident(spec_fn, *args):
    spec = spec_fn(*args)
    return pl.BlockSpec(spec.block_shape, spec.index_map, pipeline_mode=pl.Buffered(1))


def _mix_ffn(o_mix, mq, mk, mv, layer, q_gain_t, head_mean, w_out, w_layer, x, ffn_gain, w_up, w_down):
    b, t, d = x.shape
    w_mix = o_mix.shape[-1]
    mw = mq.shape[-1]
    tm = min(ROW_TILE, t)
    tok = lambda n: pl.BlockSpec((None, tm, n), lambda bi, i: (bi, i, 0))
    kv = pl.BlockSpec((None, None) + mk.shape[2:], lambda bi, i: (layer, bi, 0, 0))
    return pl.pallas_call(
        _mix_ffn_kernel,
        out_shape=jax.ShapeDtypeStruct((b, t, d), F32),
        grid=(b, t // tm),
        in_specs=[tok(w_mix), tok(mw), kv, kv, _layer_spec(q_gain_t, layer), _const_spec((mw, mw)),
                  _resident(_layer_spec, w_out, w_layer), tok(d), _layer_spec(ffn_gain, layer),
                  _resident(_layer_spec, w_up, layer), _resident(_layer_spec, w_down, layer)],
        out_specs=tok(d),
        scratch_shapes=[pltpu.VMEM((tm, d), F32)],
        compiler_params=_params("parallel", "parallel"),
        name="mix_ffn",
    )(o_mix, mq, mk, mv, q_gain_t, head_mean, w_out, x, ffn_gain, w_up, w_down)


CONV_PAD = SUBLANES


def _cat3(parts):
    hi, lo = parts
    return jnp.concatenate([hi, lo, hi], axis=1)


def _unit_lower_inverses(lowers, eye):
    r = eye.shape[0]
    ps = [-lower for lower in lowers]
    invs = [eye + p for p in ps]

    def split_rhs(p):
        parts = _split_bf16(p, 2)
        return parts, jnp.concatenate([parts[0], parts[0], parts[1]], axis=0)

    ps = [jnp.dot(_cat3(parts), rhs, preferred_element_type=F32) for parts, rhs in map(split_rhs, ps)]
    span = 4
    while span <= CHUNK:
        nxt_p, nxt_inv = [], []
        for p, inv in zip(ps, invs):
            parts, rhs = split_rhs(p)
            i_cat = _cat3(_split_bf16(inv, 2))
            if span < CHUNK:
                both = jnp.dot(jnp.concatenate([_cat3(parts), i_cat], axis=0), rhs, preferred_element_type=F32)
                nxt_p.append(both[:r])
                nxt_inv.append(inv + both[r:])
            else:
                nxt_inv.append(inv + jnp.dot(i_cat, rhs, preferred_element_type=F32))
        ps, invs = nxt_p, nxt_inv
        span *= 2

    def cat6_lhs(x):
        a0, a1, a2 = _split_bf16(x, 3)
        return jnp.concatenate([a0, a0, a0, a1, a1, a2], axis=1)

    def cat6_rhs(x):
        b0, b1, b2 = _split_bf16(x, 3)
        return jnp.concatenate([b0, b1, b2, b0, b1, b0], axis=0)

    resids = [eye - inv - jnp.dot(cat6_lhs(lower), cat6_rhs(inv), preferred_element_type=F32)
              for lower, inv in zip(lowers, invs)]
    return [inv + _dot(inv, resid) for inv, resid in zip(invs, resids)]


def _gdn_kernel(x_ref, gx_ref, w_ref, buf_ref, s0_ref, cw_ref, nal_ref, dt_ref, og_ref,
                o_ref, mq_ref, nbuf_ref, sout_ref,
                xp_ref, act_ref, z_ref, s_ref, gc_ref, kwqw_ref, ku_ref, au_ref, *, heads):
    blk = pl.program_id(1)
    tb = x_ref.shape[0]
    hd = GDN_HEAD_DIM
    width = heads * hd
    conv_ch = 3 * width
    tail = GDN_CONV - 1
    group = min(tb, GDN_GROUP)
    n_sub = group // CHUNK

    @pl.when(blk == 0)
    def _():
        xp_ref[CONV_PAD - tail:CONV_PAD, :] = buf_ref[...]
        s_ref[...] = s0_ref[...]

    xb = _rms(x_ref[...], gx_ref[...]).astype(BF16)

    def project(c0, c1):
        return jnp.dot(xb, w_ref[:, c0:c1], preferred_element_type=F32)

    for c0 in range(0, conv_ch, COL_CHUNK):
        c1 = min(c0 + COL_CHUNK, conv_ch)
        xp_ref[CONV_PAD:CONV_PAD + tb, c0:c1] = project(c0, c1)
        for ct in range(c0 // hd, c1 // hd):
            cs = slice(ct * hd, (ct + 1) * hd)
            y = None
            for j in range(GDN_CONV):
                r0 = CONV_PAD - tail + j
                term = xp_ref[r0:r0 + tb, cs] * cw_ref[j:j + 1, cs]
                y = term if y is None else y + term
            y = y * _sigmoid(y)
            if ct < 2 * heads:
                y = y * lax.rsqrt(jnp.sum(y * y, axis=-1, keepdims=True) + NORM_EPS)
                if ct < heads:
                    y = y * (hd ** -0.5)
            act_ref[:, cs] = y
    for c0 in range(0, width, COL_CHUNK):
        c1 = min(c0 + COL_CHUNK, width)
        z_ref[:, c0:c1] = project(conv_ch + c0, conv_ch + c1)
    ba = project(conv_ch + width, conv_ch + width + LANES)
    mq_ref[...] = project(conv_ch + width + LANES, w_ref.shape[-1])

    new_tail = xp_ref[CONV_PAD + tb - tail:CONV_PAD + tb, :]
    xp_ref[CONV_PAD - tail:CONV_PAD, :] = new_tail

    beta_all = _sigmoid(ba)
    g_all = nal_ref[...] * _softplus(ba + dt_ref[...])

    row = lax.broadcasted_iota(jnp.int32, (group, group), 0)
    col = lax.broadcasted_iota(jnp.int32, (group, group), 1)
    same = (row // CHUNK) == (col // CHUNK)
    causal = jnp.logical_and(same, row >= col)
    strict = jnp.logical_and(same, row > col)
    tri3 = jnp.concatenate([causal.astype(BF16)] * 3, axis=1)
    upper_f = jnp.logical_and(same, row <= col).astype(F32)
    eye = (row == col).astype(F32)
    sub_of_row = lax.broadcasted_iota(jnp.int32, (group, hd), 0) // CHUNK

    pairs = [(gi, h) for gi in range(tb // group) for h in range(heads)]
    gcs = []
    for gi in range(tb // group):
        rs = slice(gi * group, (gi + 1) * group)
        gc = jnp.dot(tri3, jnp.concatenate(_split_bf16(g_all[rs, :], 3), axis=0), preferred_element_type=F32)
        gc_ref[rs, :] = gc
        gcs.append(gc)

    def head_slices(gi, h):
        rs = slice(gi * group, (gi + 1) * group)
        return (act_ref[rs, h * hd:(h + 1) * hd], act_ref[rs, width + h * hd:width + (h + 1) * hd],
                act_ref[rs, 2 * width + h * hd:2 * width + (h + 1) * hd])

    def gate_columns(gi, h):
        gcol = gcs[gi][:, heads + h:heads + h + 1]
        bcol = beta_all[gi * group:(gi + 1) * group, h:h + 1]
        return gcol, bcol

    lowers, a_intras = [], []
    for gi, h in pairs:
        rs = slice(gi * group, (gi + 1) * group)
        gcol, bcol = gate_columns(gi, h)
        grow = jnp.sum(g_all[rs, heads + h:heads + h + 1] * upper_f, axis=0, keepdims=True)
        decay = jnp.where(causal, jnp.exp(jnp.where(causal, gcol - grow, 0.0)), 0.0)
        q, k, _ = head_slices(gi, h)
        kq = _dot_nt(jnp.concatenate([k * bcol, q], axis=0), k)
        lowers.append(kq[:group] * jnp.where(strict, decay, 0.0))
        a_intras.append(kq[group:] * decay)

    invs = _unit_lower_inverses(lowers, eye)

    sols = []
    for (gi, h), inv in zip(pairs, invs):
        gcol, bcol = gate_columns(gi, h)
        _, k, v = head_slices(gi, h)
        sols.append(_dot(inv, jnp.concatenate([v * bcol, k * bcol * jnp.exp(gcol)], axis=-1)).astype(BF16))

    for (gi, h), sol, a_intra in zip(pairs, sols, a_intras):
        gcol, _ = gate_columns(gi, h)
        glast = jnp.concatenate(
            [jnp.broadcast_to(gcol[(s + 1) * CHUNK - 1:(s + 1) * CHUNK, :], (CHUNK, 1)) for s in range(n_sub)],
            axis=0)
        q, k, _ = head_slices(gi, h)
        k_dec = k * jnp.exp(glast - gcol)
        if n_sub > 1:
            k_dec = jnp.concatenate([jnp.where(sub_of_row == s, k_dec, 0.0) for s in range(n_sub)], axis=1)
        kd_sol = _dot_tn(k_dec, sol)
        a_sol = _dot(a_intra, sol)
        qw = q * jnp.exp(gcol) - a_sol[:, hd:]
        for s in range(n_sub):
            idx = (gi * n_sub + s) * heads + h
            ku_ref[idx] = kd_sol[s * hd:(s + 1) * hd, :hd]
            au_ref[idx] = a_sol[s * CHUNK:(s + 1) * CHUNK, :hd]
            kwqw_ref[idx, :hd, :] = kd_sol[s * hd:(s + 1) * hd, hd:].astype(BF16)
            kwqw_ref[idx, hd:, :] = qw[s * CHUNK:(s + 1) * CHUNK, :].astype(BF16)

    for c in range(tb // CHUNK):
        rs = slice(c * CHUNK, (c + 1) * CHUNK)
        for h in range(heads):
            idx = c * heads + h
            gl = heads + h
            s = s_ref[h]
            glast = gc_ref[(c + 1) * CHUNK - 1:(c + 1) * CHUNK, gl:gl + 1]
            r = jnp.dot(kwqw_ref[idx], s.astype(BF16), preferred_element_type=F32)
            s_ref[h] = s * jnp.exp(glast) - r[:hd] + ku_ref[idx]
            o = r[hd:] + au_ref[idx]
            zh = z_ref[rs, h * hd:(h + 1) * hd]
            o_ref[rs, h * hd:(h + 1) * hd] = _rms(o, og_ref[...]) * (zh * _sigmoid(zh))

    @pl.when(blk == pl.num_programs(1) - 1)
    def _():
        nbuf_ref[...] = new_tail
        sout_ref[...] = s_ref[...]


def _gdn_mixer(x, gain, w_in, layer, conv_buf, s0, conv_w, neg_exp_alog, dt_bias, o_gain):
    b, t, d_model = x.shape
    heads = s0.shape[1]
    hd = GDN_HEAD_DIM
    width = heads * hd
    cw = 3 * width
    mem_width = w_in.shape[-1] - cw - width - LANES
    tail = GDN_CONV - 1
    tb = min(t, GDN_BLOCK)
    n_ch = (tb // CHUNK) * heads
    tok = lambda n: pl.BlockSpec((None, tb, n), lambda bi, i: (bi, i, 0))
    return pl.pallas_call(
        functools.partial(_gdn_kernel, heads=heads),
        out_shape=[jax.ShapeDtypeStruct((b, t, width), F32),
                   jax.ShapeDtypeStruct((b, t, mem_width), F32),
                   jax.ShapeDtypeStruct((b, tail, cw), F32),
                   jax.ShapeDtypeStruct((b, heads, hd, hd), F32)],
        grid=(b, t // tb),
        in_specs=[tok(d_model), _const_spec((1, d_model)), _layer_spec(w_in, layer),
                  pl.BlockSpec((None, tail, cw), lambda bi, i: (bi, 0, 0)),
                  pl.BlockSpec((None, heads, hd, hd), lambda bi, i: (bi, 0, 0, 0)),
                  _const_spec((GDN_CONV, cw)), _const_spec((1, LANES)), _const_spec((1, LANES)),
                  _const_spec((1, hd))],
        out_specs=[tok(width), tok(mem_width),
                   pl.BlockSpec((None, tail, cw), lambda bi, i: (bi, 0, 0)),
                   pl.BlockSpec((None, heads, hd, hd), lambda bi, i: (bi, 0, 0, 0))],
        scratch_shapes=[pltpu.VMEM((CONV_PAD + tb, cw), F32), pltpu.VMEM((tb, cw), F32),
                        pltpu.VMEM((tb, width), F32),
                        pltpu.VMEM((heads, hd, hd), F32), pltpu.VMEM((tb, LANES), F32),
                        pltpu.VMEM((n_ch, hd + CHUNK, hd), BF16), pltpu.VMEM((n_ch, hd, hd), F32),
                        pltpu.VMEM((n_ch, CHUNK, hd), F32)],
        compiler_params=_params("parallel", "arbitrary"),
        name="gdn_mixer",
    )(x, gain, w_in, conv_buf, s0, conv_w, neg_exp_alog, dt_bias, o_gain)


def _suffix_sums(m, tri2):
    return jnp.dot(jnp.concatenate(_split_bf16(m, 2), axis=1), tri2, preferred_element_type=F32)


def _tri2(n):
    row = lax.broadcasted_iota(jnp.int32, (n, n), 0)
    col = lax.broadcasted_iota(jnp.int32, (n, n), 1)
    tri = (row >= col).astype(BF16)
    return jnp.concatenate([tri, tri], axis=0), row > col


def _sb_attn_kernel(q_ref, kd_ref, vd_ref, *refs, heads, tk, off_blocks, interleaved, has_prev):
    if has_prev:
        kp_ref, vp_ref = refs[:2]
        refs = refs[2:]
    ko_ref, vo_ref, gq_ref, gk_ref, o_ref, acc_ref, suf_ref = refs
    tq = q_ref.shape[0]
    d = SB_HEAD_DIM
    n_off = off_blocks(pl.program_id(1))

    tri_q, strict = _tri2(tq)
    tri_k = tri_q if tk == tq else _tri2(tk)[0]
    cols = [slice(h * d, (h + 1) * d) for h in range(heads)]
    zs = [_dot_nt(q_ref[:, cs], kd_ref[:, cs]) for cs in cols]
    cds = [_suffix_sums(jnp.where(strict, jnp.log1p(jnp.exp(z)), 0.0), tri_q) for z in zs]
    sufs = [c[:, 0:1] for c in cds]
    weights = [jnp.where(strict, jnp.exp(z - c), 0.0) for z, c in zip(zs, cds)]
    if has_prev:
        valid = n_off > 0
        zps = [_dot_nt(q_ref[:, cs], kp_ref[:, cs]) for cs in cols]
        cps = [_suffix_sums(jnp.log1p(jnp.exp(zp)), tri_k) for zp in zps]
        weights_p = [jnp.where(valid, jnp.exp(zp - cp - suf), 0.0) for zp, cp, suf in zip(zps, cps, sufs)]
        sufs = [suf + jnp.where(valid, cp[:, 0:1], 0.0) for suf, cp in zip(sufs, cps)]
    suf_min = None
    for h, cs in enumerate(cols):
        acc = _dot(weights[h], vd_ref[:, cs])
        if has_prev:
            acc = acc + _dot(weights_p[h], vp_ref[:, cs])
        acc_ref[:, cs] = acc
        suf_ref[h] = sufs[h]
        hmin = jnp.min(sufs[h])
        suf_min = hmin if suf_min is None else jnp.minimum(suf_min, hmin)

    z_bound = (jnp.max(jnp.abs(gq_ref[...])) * jnp.max(jnp.abs(gk_ref[...]))
               * (math.sqrt(SB_HEAD_DIM) * 1.01))

    def live(carry):
        n, smin = carry
        return jnp.logical_and(n >= 0, z_bound - smin > -F32_MIN_NORMAL_LOG)

    def step(carry):
        n, _ = carry
        kvs = []
        for h, cs in enumerate(cols):
            if interleaved:
                rows = pl.ds(n * (tk * heads) + h, tk, stride=heads)
                kvs.append((ko_ref[rows, :], vo_ref[rows, :]))
            else:
                rows = pl.ds(pl.multiple_of(n * tk, tk), tk)
                kvs.append((ko_ref[rows, cs], vo_ref[rows, cs]))
        zos = [_dot_nt(q_ref[:, cs], kb) for cs, (kb, _) in zip(cols, kvs)]
        cos = [_suffix_sums(jnp.log1p(jnp.exp(zo)), tri_k) for zo in zos]
        smin = None
        for h, cs in enumerate(cols):
            suf = suf_ref[h]
            acc_ref[:, cs] += _dot(jnp.exp(zos[h] - cos[h] - suf), kvs[h][1])
            suf = suf + cos[h][:, 0:1]
            suf_ref[h] = suf
            hmin = jnp.min(suf)
            smin = hmin if smin is None else jnp.minimum(smin, hmin)
        return n - 1, smin

    lax.while_loop(live, step, (n_off - (2 if has_prev else 1), suf_min))
    o_ref[...] = acc_ref[...]


def _sb_attention(q, k_new, v_new, k_off, v_off, gq, gk, tq, tk, causal_prompt, layer=0):
    b, t, width = q.shape
    d = SB_HEAD_DIM
    heads = width // d
    blk = pl.BlockSpec((None, tq, width), lambda bi, i: (bi, i, 0))
    if causal_prompt:
        full = pl.BlockSpec((None,) + k_off.shape[1:], lambda bi, i: (bi, 0, 0), pipeline_mode=pl.Buffered(1))
    else:
        full = pl.BlockSpec((None, None) + k_off.shape[2:], lambda bi, i: (layer, bi, 0, 0),
                            pipeline_mode=pl.Buffered(1))
    in_specs = [blk, blk, blk]
    args = [q, k_new, v_new]
    if causal_prompt:
        assert tq == tk
        off_blocks = lambda i: i
        prev = pl.BlockSpec((None, tk, width), lambda bi, i: (bi, jnp.maximum(i - 1, 0), 0))
        in_specs += [prev, prev]
        args += [k_off, v_off]
    else:
        t_off = k_off.shape[2] // heads
        off_blocks = lambda i: t_off // tk
    in_specs += [full, full, _const_spec((1, d)), _const_spec((1, d))]
    args += [k_off, v_off, gq, gk]
    return pl.pallas_call(
        functools.partial(_sb_attn_kernel, heads=heads, tk=tk, off_blocks=off_blocks,
                          interleaved=not causal_prompt, has_prev=causal_prompt),
        out_shape=jax.ShapeDtypeStruct((b, t, width), F32),
        grid=(b, t // tq),
        in_specs=in_specs,
        out_specs=blk,
        scratch_shapes=[pltpu.VMEM((tq, width), F32), pltpu.VMEM((heads, tq, 1), F32)],
        compiler_params=_params("parallel", "arbitrary"),
        name="sb_attention",
    )(*args)


def _head_mean_matrix(width, head_dim):
    idx = jnp.arange(width) // head_dim
    return ((idx[:, None] == idx[None, :]).astype(F32) / head_dim).astype(BF16)


def kernel(x_prompt, x_sample, state_gdn_conv, state_gdn_s, cache_sb_k, cache_sb_v, cache_mem_k, cache_mem_v,
           mem_prompt, norm_mix, norm_mem, norm_ffn, w_in_gdn, w_in_sb, w_mem_kv, mem_q_gain, mem_k_gain,
           gdn_conv_w, gdn_a_log, gdn_dt_bias, gdn_o_gain, sb_q_gain, sb_k_gain, w_out_gdn, w_out_sb, w_up, w_down):
    depth, d_model = norm_mix.shape
    bp, t_p, _ = x_prompt.shape
    bs, t_s, _ = x_sample.shape
    gdn_heads = gdn_a_log.shape[1]
    gdn_width = gdn_heads * GDN_HEAD_DIM
    conv_ch = gdn_conv_w.shape[-1]
    n_sb, _, past, sb_heads, _ = cache_sb_k.shape
    sb_width = sb_heads * SB_HEAD_DIM
    mem_tokens, mem_heads = cache_mem_k.shape[2], cache_mem_k.shape[3]
    mem_width = mem_heads * MEM_HEAD_DIM
    assert MEM_HEAD_DIM & (MEM_HEAD_DIM - 1) == 0

    n_gate = 2 * gdn_heads
    tok_in = conv_ch + gdn_width + n_gate
    w_g = w_in_gdn
    w_gdn = jnp.concatenate(
        [w_g[..., :conv_ch + gdn_width],
         jnp.pad(w_g[..., conv_ch + gdn_width:tok_in], ((0, 0), (0, 0), (0, LANES - n_gate))),
         w_g[..., tok_in:]], axis=-1).astype(BF16)
    w_sb = w_in_sb.astype(BF16)
    w_kv = w_mem_kv.astype(BF16)
    w_og = w_out_gdn.astype(BF16)
    w_os = w_out_sb.astype(BF16)
    w_u = w_up.astype(BF16)
    w_d = w_down.astype(BF16)

    head_mean = _head_mean_matrix(mem_width, MEM_HEAD_DIM)
    q_gain_t = jnp.tile(mem_q_gain, (1, mem_heads))[:, None, :]
    k_gain_t = jnp.tile(mem_k_gain, (1, mem_heads))[:, None, :]
    lane_pad = lambda a: jnp.pad(a, ((0, 0), (gdn_heads, LANES - n_gate)))[:, None, :]
    neg_exp_alog = lane_pad(-jnp.exp(gdn_a_log))
    dt_bias = lane_pad(gdn_dt_bias)

    pmk, pmv = _mem_kv(mem_prompt, norm_mem[:, None, :], w_kv, k_gain_t, head_mean)
    pmk5 = pmk.reshape(depth, bp, mem_tokens, mem_heads, MEM_HEAD_DIM)
    pmv5 = pmv.reshape(depth, bp, mem_tokens, mem_heads, MEM_HEAD_DIM)
    smk = cache_mem_k.reshape(depth, bs, mem_tokens, mem_width)
    smv = cache_mem_v.reshape(depth, bs, mem_tokens, mem_width)
    cache_k = cache_sb_k.reshape(n_sb, bs, past * sb_heads, SB_HEAD_DIM)
    cache_v = cache_sb_v.reshape(n_sb, bs, past * sb_heads, SB_HEAD_DIM)

    yp, ys = x_prompt, x_sample
    pc, pst, sc, sst = [], [], [], []
    kv_stacks = {prompt: tuple(jnp.zeros((n_sb, rows * sb_heads, SB_HEAD_DIM), F32) for _ in range(2))
                 for prompt, rows in ((True, bp * t_p), (False, bs * t_s))}
    for i in range(depth):
        j = i // 2
        g_mix = norm_mix[i][None, :]
        paths = []
        for y, mk, mv, is_prompt in ((yp, pmk, pmv, True), (ys, smk, smv, False)):
            b, t, _ = y.shape
            y2 = y.reshape(b * t, d_model)
            if i % 2 == 0:
                if is_prompt:
                    buf0 = jnp.zeros((b, GDN_CONV - 1, conv_ch), F32)
                    st0 = jnp.zeros((b, gdn_heads, GDN_HEAD_DIM, GDN_HEAD_DIM), F32)
                else:
                    buf0, st0 = state_gdn_conv[j], state_gdn_s[j]
                o_mix, mq, nbuf, s_new = _gdn_mixer(y, g_mix, w_gdn, j, buf0, st0, gdn_conv_w[j],
                                                    neg_exp_alog[j], dt_bias[j], gdn_o_gain[j][None, :])
                (pc if is_prompt else sc).append(nbuf)
                (pst if is_prompt else sst).append(s_new)
                w_o = w_og
            else:
                gq, gk = sb_q_gain[j][None, :], sb_k_gain[j][None, :]
                q, kb, vb, mq, k_all, v_all = _in_proj_sb(y2, g_mix, w_sb, gq, gk, sb_heads, mem_width,
                                                          j, kv_stacks[is_prompt])
                kv_stacks[is_prompt] = (k_all, v_all)
                r3 = lambda a: a.reshape(b, t, sb_width)
                if is_prompt:
                    tq = min(t, SB_QUERY_BLOCK)
                    o_mix = _sb_attention(r3(q), r3(kb), r3(vb), r3(kb), r3(vb), gq, gk, tq, tq, True)
                else:
                    o_mix = _sb_attention(r3(q), r3(kb), r3(vb), cache_k, cache_v, gq, gk, t,
                                          min(past, SB_CACHE_BLOCK), False, j)
                w_o = w_os
            y = _mix_ffn(o_mix, mq.reshape(b, t, mem_width), mk, mv, i, q_gain_t, head_mean, w_o, j, y,
                         norm_ffn[:, None, :], w_u, w_d)
            paths.append(y)
        yp, ys = paths
    heads5 = lambda a, b, t: a.reshape(n_sb, b, t, sb_heads, SB_HEAD_DIM)
    pk, pv = (heads5(a, bp, t_p) for a in kv_stacks[True])
    sk, sv = (heads5(a, bs, t_s) for a in kv_stacks[False])
    return (yp, ys, jnp.stack(pc), jnp.stack(pst), pk, pv, pmk5, pmv5,
            jnp.stack(sc), jnp.stack(sst), sk, sv)
```

```python
import functools
import math

import jax
import jax.numpy as jnp
from jax import lax
from jax.experimental import pallas as pl
from jax.experimental.pallas import tpu as pltpu

F32 = jnp.float32
BF16 = jnp.bfloat16

NORM_EPS = 1e-6
CHUNK = 64
GDN_HEAD_DIM = 128
GDN_CONV = 4
GDN_BLOCK = 256
GDN_GROUP = 128
SB_HEAD_DIM = 128
SB_QUERY_BLOCK = 256
SB_CACHE_BLOCK = 256
MEM_HEAD_DIM = 64
LANES = 128
SUBLANES = 8
VMEM_LIMIT_BYTES = 56 * 1024 * 1024
ROW_TILE = 512
COL_CHUNK = 512
FF_CHUNK = 1024
F32_MIN_NORMAL_LOG = 88.0


def _params(*semantics):
    return pltpu.CompilerParams(dimension_semantics=semantics, vmem_limit_bytes=VMEM_LIMIT_BYTES)


def _dot(a, b):
    return jnp.dot(a.astype(BF16), b.astype(BF16), preferred_element_type=F32)


def _dot_nt(a, b):
    return lax.dot_general(a.astype(BF16), b.astype(BF16), (((1,), (1,)), ((), ())),
                           preferred_element_type=F32)


def _dot_tn(a, b):
    return lax.dot_general(a.astype(BF16), b.astype(BF16), (((0,), (0,)), ((), ())),
                           preferred_element_type=F32)


def _split_bf16(x, terms):
    parts = []
    for _ in range(terms):
        p = x.astype(BF16)
        parts.append(p)
        x = x - p.astype(F32)
    return parts


def _dot_exact_rhs(a, b_exact, terms):
    acc = None
    for p in _split_bf16(a, terms):
        t = jnp.dot(p, b_exact, preferred_element_type=F32)
        acc = t if acc is None else acc + t
    return acc


def _rms(x, gain):
    return x * lax.rsqrt(jnp.mean(x * x, axis=-1, keepdims=True) + NORM_EPS) * gain


def _sigmoid(x):
    return 1.0 / (1.0 + jnp.exp(-x))


def _softplus(x):
    return jnp.maximum(x, 0.0) + jnp.log1p(jnp.exp(-jnp.abs(x)))


def _const_spec(shape):
    zeros = (0,) * len(shape)
    return pl.BlockSpec(shape, lambda *_: zeros)


def _row_spec(tm, n):
    return pl.BlockSpec((tm, n), lambda i: (i, 0))


def _layer_spec(stacked, layer):
    zeros = (0,) * (stacked.ndim - 1)
    return pl.BlockSpec((None,) + stacked.shape[1:], lambda *_: (layer,) + zeros)


def _project(xb, w_ref, o_ref, off, n):
    for c in range(0, n, COL_CHUNK):
        cw = min(COL_CHUNK, n - c)
        o_ref[:, c:c + cw] = jnp.dot(xb, w_ref[:, off + c:off + c + cw],
                                     preferred_element_type=F32).astype(o_ref.dtype)


def _in_proj_sb_kernel(x_ref, g_ref, w_ref, gq_ref, gk_ref, k_in_ref, v_in_ref,
                       q_ref, kb_ref, vb_ref, mq_ref, k_ref, v_ref, *, heads):
    del k_in_ref, v_in_ref
    xb = _rms(x_ref[...], g_ref[...]).astype(BF16)
    tm = x_ref.shape[0]
    d = SB_HEAD_DIM
    width = heads * d
    q_all = jnp.dot(xb, w_ref[:, :width], preferred_element_type=F32)
    k_all = jnp.dot(xb, w_ref[:, width:2 * width], preferred_element_type=F32)
    v_all = jnp.dot(xb, w_ref[:, 2 * width:3 * width], preferred_element_type=F32)
    vb_ref[...] = v_all.astype(BF16)
    for h in range(heads):
        cs = slice(h * d, (h + 1) * d)
        interleaved = pl.ds(h, tm, stride=heads)
        q_ref[:, cs] = (_rms(q_all[:, cs], gq_ref[...]) * (d ** -0.5)).astype(BF16)
        k = _rms(k_all[:, cs], gk_ref[...])
        k_ref[interleaved, :] = k
        kb_ref[:, cs] = k.astype(BF16)
        v_ref[interleaved, :] = v_all[:, cs]
    _project(xb, w_ref, mq_ref, 3 * width, mq_ref.shape[-1])


def _in_proj_sb(x2, gain, w, gq, gk, heads, mem_width, slot, kv_all):
    rows, d_model = x2.shape
    d = SB_HEAD_DIM
    width = heads * d
    tm = min(ROW_TILE, rows)
    stack_shape = jax.ShapeDtypeStruct(kv_all[0].shape, F32)
    stack_spec = pl.BlockSpec((None, tm * heads, d), lambda i: (slot, i, 0))
    return pl.pallas_call(
        functools.partial(_in_proj_sb_kernel, heads=heads),
        out_shape=[jax.ShapeDtypeStruct((rows, width), BF16)] * 3
        + [jax.ShapeDtypeStruct((rows, mem_width), F32), stack_shape, stack_shape],
        grid=(rows // tm,),
        in_specs=[_row_spec(tm, d_model), _const_spec((1, d_model)), _layer_spec(w, slot),
                  _const_spec((1, d)), _const_spec((1, d)),
                  pl.BlockSpec(memory_space=pl.ANY), pl.BlockSpec(memory_space=pl.ANY)],
        out_specs=[_row_spec(tm, width)] * 3 + [_row_spec(tm, mem_width), stack_spec, stack_spec],
        input_output_aliases={5: 4, 6: 5},
        compiler_params=_params("parallel"),
        name="in_proj_sb",
    )(x2, gain, w, gq, gk, *kv_all)


def _head_mean_sq(x, hm_ref):
    return _dot_exact_rhs(x * x, hm_ref[...], 2)


def _mem_kv_kernel(mem_ref, g_ref, w_ref, kg_ref, hm_ref, k_ref, v_ref):
    xb = _rms(mem_ref[...], g_ref[...]).astype(BF16)
    width = k_ref.shape[-1]
    k = jnp.dot(xb, w_ref[:, :width], preferred_element_type=F32)
    k_ref[...] = k * lax.rsqrt(_head_mean_sq(k, hm_ref) + NORM_EPS) * kg_ref[...]
    v_ref[...] = jnp.dot(xb, w_ref[:, width:], preferred_element_type=F32)


def _mem_kv(mem, norm_mem, w_kv, k_gain_t, head_mean):
    bp, m, d = mem.shape
    depth = w_kv.shape[0]
    width = w_kv.shape[-1] // 2
    return pl.pallas_call(
        _mem_kv_kernel,
        out_shape=[jax.ShapeDtypeStruct((depth, bp, m, width), F32)] * 2,
        grid=(depth, bp),
        in_specs=[pl.BlockSpec((None, m, d), lambda i, b: (b, 0, 0)),
                  pl.BlockSpec((None, 1, d), lambda i, b: (i, 0, 0)),
                  pl.BlockSpec((None, d, 2 * width), lambda i, b: (i, 0, 0)),
                  pl.BlockSpec((None, 1, width), lambda i, b: (i, 0, 0)),
                  _const_spec((width, width))],
        out_specs=[pl.BlockSpec((None, None, m, width), lambda i, b: (i, b, 0, 0))] * 2,
        compiler_params=_params("parallel", "parallel"),
        name="mem_kv",
    )(mem, norm_mem, w_kv, k_gain_t, head_mean)


def _mix_ffn_kernel(o_ref, mq_ref, k_ref, v_ref, qg_ref, hm_ref, wo_ref, x_ref, gf_ref, wu_ref, wd_ref,
                    y_ref, ymid_ref):
    q = mq_ref[...]
    mw = q.shape[-1]
    qn = q * lax.rsqrt(_head_mean_sq(q, hm_ref) + NORM_EPS) * qg_ref[...]
    kb = k_ref[...].astype(BF16)
    v = v_ref[...]
    lane_head = lax.broadcasted_iota(jnp.int32, (1, mw), 1) // MEM_HEAD_DIM
    scores = [_dot_nt(jnp.where(lane_head == h, qn, 0.0), kb) * (MEM_HEAD_DIM ** -0.5)
              for h in range(mw // MEM_HEAD_DIM)]
    mem = None
    for h, sh in enumerate(scores):
        e = jnp.exp(sh - jnp.max(sh, axis=-1, keepdims=True))
        p = e / jnp.sum(e, axis=-1, keepdims=True)
        part = _dot(p, jnp.where(lane_head == h, v, 0.0))
        mem = part if mem is None else mem + part
    w_mix = o_ref.shape[-1]
    ymid_ref[...] = x_ref[...] + _dot(o_ref[...], wo_ref[:w_mix, :]) + _dot(mem, wo_ref[w_mix:, :])

    hb = _rms(ymid_ref[...], gf_ref[...]).astype(BF16)
    acc = None
    for c in range(0, wu_ref.shape[-1], FF_CHUNK):
        a = jnp.maximum(jnp.dot(hb, wu_ref[:, c:c + FF_CHUNK], preferred_element_type=F32), 0.0)
        t = jnp.dot((a * a).astype(BF16), wd_ref[c:c + FF_CHUNK, :], preferred_element_type=F32)
        acc = t if acc is None else acc + t
    y_ref[...] = ymid_ref[...] + acc


def _resident(spec_fn, *args):
    spec = spec_fn(*args)
    return pl.BlockSpec(spec.block_shape, spec.index_map, pipeline_mode=pl.Buffered(1))


def _mix_ffn(o_mix, mq, mk, mv, layer, q_gain_t, head_mean, w_out, w_layer, x, ffn_gain, w_up, w_down):
    b, t, d = x.shape
    w_mix = o_mix.shape[-1]
    mw = mq.shape[-1]
    tm = min(ROW_TILE, t)
    tok = lambda n: pl.BlockSpec((None, tm, n), lambda bi, i: (bi, i, 0))
    kv = pl.BlockSpec((None, None) + mk.shape[2:], lambda bi, i: (layer, bi, 0, 0))
    return pl.pallas_call(
        _mix_ffn_kernel,
        out_shape=jax.ShapeDtypeStruct((b, t, d), F32),
        grid=(b, t // tm),
        in_specs=[tok(w_mix), tok(mw), kv, kv, _layer_spec(q_gain_t, layer), _const_spec((mw, mw)),
                  _resident(_layer_spec, w_out, w_layer), tok(d), _layer_spec(ffn_gain, layer),
                  _resident(_layer_spec, w_up, layer), _resident(_layer_spec, w_down, layer)],
        out_specs=tok(d),
        scratch_shapes=[pltpu.VMEM((tm, d), F32)],
        compiler_params=_params("parallel", "parallel"),
        name="mix_ffn",
    )(o_mix, mq, mk, mv, q_gain_t, head_mean, w_out, x, ffn_gain, w_up, w_down)


CONV_PAD = SUBLANES


def _cat3(parts):
    hi, lo = parts
    return jnp.concatenate([hi, lo, hi], axis=1)


def _unit_lower_inverses(lowers, eye):
    r = eye.shape[0]
    ps = [-lower for lower in lowers]
    invs = [eye + p for p in ps]

    def split_rhs(p):
        parts = _split_bf16(p, 2)
        return parts, jnp.concatenate([parts[0], parts[0], parts[1]], axis=0)

    ps = [jnp.dot(_cat3(parts), rhs, preferred_element_type=F32) for parts, rhs in map(split_rhs, ps)]
    span = 4
    while span <= CHUNK:
        nxt_p, nxt_inv = [], []
        for p, inv in zip(ps, invs):
            parts, rhs = split_rhs(p)
            i_cat = _cat3(_split_bf16(inv, 2))
            if span < CHUNK:
                both = jnp.dot(jnp.concatenate([_cat3(parts), i_cat], axis=0), rhs, preferred_element_type=F32)
                nxt_p.append(both[:r])
                nxt_inv.append(inv + both[r:])
            else:
                nxt_inv.append(inv + jnp.dot(i_cat, rhs, preferred_element_type=F32))
        ps, invs = nxt_p, nxt_inv
        span *= 2

    def cat6_lhs(x):
        a0, a1, a2 = _split_bf16(x, 3)
        return jnp.concatenate([a0, a0, a0, a1, a1, a2], axis=1)

    def cat6_rhs(x):
        b0, b1, b2 = _split_bf16(x, 3)
        return jnp.concatenate([b0, b1, b2, b0, b1, b0], axis=0)

    resids = [eye - inv - jnp.dot(cat6_lhs(lower), cat6_rhs(inv), preferred_element_type=F32)
              for lower, inv in zip(lowers, invs)]
    return [inv + _dot(inv, resid) for inv, resid in zip(invs, resids)]


def _gdn_kernel(x_ref, gx_ref, w_ref, buf_ref, s0_ref, cw_ref, nal_ref, dt_ref, og_ref,
                o_ref, mq_ref, nbuf_ref, sout_ref,
                xp_ref, act_ref, z_ref, s_ref, gc_ref, kwqw_ref, ku_ref, au_ref, *, heads):
    blk = pl.program_id(1)
    tb = x_ref.shape[0]
    hd = GDN_HEAD_DIM
    width = heads * hd
    conv_ch = 3 * width
    tail = GDN_CONV - 1
    group = min(tb, GDN_GROUP)
    n_sub = group // CHUNK

    @pl.when(blk == 0)
    def _():
        xp_ref[CONV_PAD - tail:CONV_PAD, :] = buf_ref[...]
        s_ref[...] = s0_ref[...]

    xb = _rms(x_ref[...], gx_ref[...]).astype(BF16)

    def project(c0, c1):
        return jnp.dot(xb, w_ref[:, c0:c1], preferred_element_type=F32)

    for c0 in range(0, conv_ch, COL_CHUNK):
        c1 = min(c0 + COL_CHUNK, conv_ch)
        xp_ref[CONV_PAD:CONV_PAD + tb, c0:c1] = project(c0, c1)
        for ct in range(c0 // hd, c1 // hd):
            cs = slice(ct * hd, (ct + 1) * hd)
            y = None
            for j in range(GDN_CONV):
                r0 = CONV_PAD - tail + j
                term = xp_ref[r0:r0 + tb, cs] * cw_ref[j:j + 1, cs]
                y = term if y is None else y + term
            y = y * _sigmoid(y)
            if ct < 2 * heads:
                y = y * lax.rsqrt(jnp.sum(y * y, axis=-1, keepdims=True) + NORM_EPS)
                if ct < heads:
                    y = y * (hd ** -0.5)
            act_ref[:, cs] = y
    for c0 in range(0, width, COL_CHUNK):
        c1 = min(c0 + COL_CHUNK, width)
        z_ref[:, c0:c1] = project(conv_ch + c0, conv_ch + c1)
    ba = project(conv_ch + width, conv_ch + width + LANES)
    mq_ref[...] = project(conv_ch + width + LANES, w_ref.shape[-1])

    new_tail = xp_ref[CONV_PAD + tb - tail:CONV_PAD + tb, :]
    xp_ref[CONV_PAD - tail:CONV_PAD, :] = new_tail

    beta_all = _sigmoid(ba)
    g_all = nal_ref[...] * _softplus(ba + dt_ref[...])

    row = lax.broadcasted_iota(jnp.int32, (group, group), 0)
    col = lax.broadcasted_iota(jnp.int32, (group, group), 1)
    same = (row // CHUNK) == (col // CHUNK)
    causal = jnp.logical_and(same, row >= col)
    strict = jnp.logical_and(same, row > col)
    tri3 = jnp.concatenate([causal.astype(BF16)] * 3, axis=1)
    upper_f = jnp.logical_and(same, row <= col).astype(F32)
    eye = (row == col).astype(F32)
    sub_of_row = lax.broadcasted_iota(jnp.int32, (group, hd), 0) // CHUNK

    pairs = [(gi, h) for gi in range(tb // group) for h in range(heads)]
    gcs = []
    for gi in range(tb // group):
        rs = slice(gi * group, (gi + 1) * group)
        gc = jnp.dot(tri3, jnp.concatenate(_split_bf16(g_all[rs, :], 3), axis=0), preferred_element_type=F32)
        gc_ref[rs, :] = gc
        gcs.append(gc)

    def head_slices(gi, h):
        rs = slice(gi * group, (gi + 1) * group)
        return (act_ref[rs, h * hd:(h + 1) * hd], act_ref[rs, width + h * hd:width + (h + 1) * hd],
                act_ref[rs, 2 * width + h * hd:2 * width + (h + 1) * hd])

    def gate_columns(gi, h):
        gcol = gcs[gi][:, heads + h:heads + h + 1]
        bcol = beta_all[gi * group:(gi + 1) * group, h:h + 1]
        return gcol, bcol

    lowers, a_intras = [], []
    for gi, h in pairs:
        rs = slice(gi * group, (gi + 1) * group)
        gcol, bcol = gate_columns(gi, h)
        grow = jnp.sum(g_all[rs, heads + h:heads + h + 1] * upper_f, axis=0, keepdims=True)
        decay = jnp.where(causal, jnp.exp(jnp.where(causal, gcol - grow, 0.0)), 0.0)
        q, k, _ = head_slices(gi, h)
        kq = _dot_nt(jnp.concatenate([k * bcol, q], axis=0), k)
        lowers.append(kq[:group] * jnp.where(strict, decay, 0.0))
        a_intras.append(kq[group:] * decay)

    invs = _unit_lower_inverses(lowers, eye)

    sols = []
    for (gi, h), inv in zip(pairs, invs):
        gcol, bcol = gate_columns(gi, h)
        _, k, v = head_slices(gi, h)
        sols.append(_dot(inv, jnp.concatenate([v * bcol, k * bcol * jnp.exp(gcol)], axis=-1)).astype(BF16))

    for (gi, h), sol, a_intra in zip(pairs, sols, a_intras):
        gcol, _ = gate_columns(gi, h)
        glast = jnp.concatenate(
            [jnp.broadcast_to(gcol[(s + 1) * CHUNK - 1:(s + 1) * CHUNK, :], (CHUNK, 1)) for s in range(n_sub)],
            axis=0)
        q, k, _ = head_slices(gi, h)
        k_dec = k * jnp.exp(glast - gcol)
        if n_sub > 1:
            k_dec = jnp.concatenate([jnp.where(sub_of_row == s, k_dec, 0.0) for s in range(n_sub)], axis=1)
        kd_sol = _dot_tn(k_dec, sol)
        a_sol = _dot(a_intra, sol)
        qw = q * jnp.exp(gcol) - a_sol[:, hd:]
        for s in range(n_sub):
            idx = (gi * n_sub + s) * heads + h
            ku_ref[idx] = kd_sol[s * hd:(s + 1) * hd, :hd]
            au_ref[idx] = a_sol[s * CHUNK:(s + 1) * CHUNK, :hd]
            kwqw_ref[idx, :hd, :] = kd_sol[s * hd:(s + 1) * hd, hd:].astype(BF16)
            kwqw_ref[idx, hd:, :] = qw[s * CHUNK:(s + 1) * CHUNK, :].astype(BF16)

    for c in range(tb // CHUNK):
        rs = slice(c * CHUNK, (c + 1) * CHUNK)
        for h in range(heads):
            idx = c * heads + h
            gl = heads + h
            s = s_ref[h]
            glast = gc_ref[(c + 1) * CHUNK - 1:(c + 1) * CHUNK, gl:gl + 1]
            r = jnp.dot(kwqw_ref[idx], s.astype(BF16), preferred_element_type=F32)
            s_ref[h] = s * jnp.exp(glast) - r[:hd] + ku_ref[idx]
            o = r[hd:] + au_ref[idx]
            zh = z_ref[rs, h * hd:(h + 1) * hd]
            o_ref[rs, h * hd:(h + 1) * hd] = _rms(o, og_ref[...]) * (zh * _sigmoid(zh))

    @pl.when(blk == pl.num_programs(1) - 1)
    def _():
        nbuf_ref[...] = new_tail
        sout_ref[...] = s_ref[...]


def _gdn_mixer(x, gain, w_in, layer, conv_buf, s0, conv_w, neg_exp_alog, dt_bias, o_gain):
    b, t, d_model = x.shape
    heads = s0.shape[1]
    hd = GDN_HEAD_DIM
    width = heads * hd
    cw = 3 * width
    mem_width = w_in.shape[-1] - cw - width - LANES
    tail = GDN_CONV - 1
    tb = min(t, GDN_BLOCK)
    n_ch = (tb // CHUNK) * heads
    tok = lambda n: pl.BlockSpec((None, tb, n), lambda bi, i: (bi, i, 0))
    return pl.pallas_call(
        functools.partial(_gdn_kernel, heads=heads),
        out_shape=[jax.ShapeDtypeStruct((b, t, width), F32),
                   jax.ShapeDtypeStruct((b, t, mem_width), F32),
                   jax.ShapeDtypeStruct((b, tail, cw), F32),
                   jax.ShapeDtypeStruct((b, heads, hd, hd), F32)],
        grid=(b, t // tb),
        in_specs=[tok(d_model), _const_spec((1, d_model)), _layer_spec(w_in, layer),
                  pl.BlockSpec((None, tail, cw), lambda bi, i: (bi, 0, 0)),
                  pl.BlockSpec((None, heads, hd, hd), lambda bi, i: (bi, 0, 0, 0)),
                  _const_spec((GDN_CONV, cw)), _const_spec((1, LANES)), _const_spec((1, LANES)),
                  _const_spec((1, hd))],
        out_specs=[tok(width), tok(mem_width),
                   pl.BlockSpec((None, tail, cw), lambda bi, i: (bi, 0, 0)),
                   pl.BlockSpec((None, heads, hd, hd), lambda bi, i: (bi, 0, 0, 0))],
        scratch_shapes=[pltpu.VMEM((CONV_PAD + tb, cw), F32), pltpu.VMEM((tb, cw), F32),
                        pltpu.VMEM((tb, width), F32),
                        pltpu.VMEM((heads, hd, hd), F32), pltpu.VMEM((tb, LANES), F32),
                        pltpu.VMEM((n_ch, hd + CHUNK, hd), BF16), pltpu.VMEM((n_ch, hd, hd), F32),
                        pltpu.VMEM((n_ch, CHUNK, hd), F32)],
        compiler_params=_params("parallel", "arbitrary"),
        name="gdn_mixer",
    )(x, gain, w_in, conv_buf, s0, conv_w, neg_exp_alog, dt_bias, o_gain)


def _softplus_logit(z):
    return jnp.log(1.0 + jnp.exp(z))


def _suffix_sums(m, tri2):
    return jnp.dot(jnp.concatenate(_split_bf16(m, 2), axis=1), tri2, preferred_element_type=F32)


def _tri2(n):
    row = lax.broadcasted_iota(jnp.int32, (n, n), 0)
    col = lax.broadcasted_iota(jnp.int32, (n, n), 1)
    tri = (row >= col).astype(BF16)
    return jnp.concatenate([tri, tri], axis=0), row > col


def _sb_attn_kernel(q_ref, kd_ref, vd_ref, *refs, heads, tk, off_blocks, interleaved, has_prev):
    if has_prev:
        kp_ref, vp_ref = refs[:2]
        refs = refs[2:]
    ko_ref, vo_ref, gq_ref, gk_ref, o_ref, acc_ref, suf_ref = refs
    tq = q_ref.shape[0]
    d = SB_HEAD_DIM
    n_off = off_blocks(pl.program_id(1))

    tri_q, strict = _tri2(tq)
    tri_k = tri_q if tk == tq else _tri2(tk)[0]
    cols = [slice(h * d, (h + 1) * d) for h in range(heads)]
    zs = [_dot_nt(q_ref[:, cs], kd_ref[:, cs]) for cs in cols]
    cds = [_suffix_sums(jnp.where(strict, _softplus_logit(z), 0.0), tri_q) for z in zs]
    sufs = [c[:, 0:1] for c in cds]
    weights = [jnp.where(strict, jnp.exp(z - c), 0.0) for z, c in zip(zs, cds)]
    if has_prev:
        valid = n_off > 0
        zps = [_dot_nt(q_ref[:, cs], kp_ref[:, cs]) for cs in cols]
        cps = [_suffix_sums(_softplus_logit(zp), tri_k) for zp in zps]
        weights_p = [jnp.where(valid, jnp.exp(zp - cp - suf), 0.0) for zp, cp, suf in zip(zps, cps, sufs)]
        sufs = [suf + jnp.where(valid, cp[:, 0:1], 0.0) for suf, cp in zip(sufs, cps)]
    suf_min = None
    for h, cs in enumerate(cols):
        acc = _dot(weights[h], vd_ref[:, cs])
        if has_prev:
            acc = acc + _dot(weights_p[h], vp_ref[:, cs])
        acc_ref[:, cs] = acc
        suf_ref[h] = sufs[h]
        hmin = jnp.min(sufs[h])
        suf_min = hmin if suf_min is None else jnp.minimum(suf_min, hmin)

    z_bound = (jnp.max(jnp.abs(gq_ref[...])) * jnp.max(jnp.abs(gk_ref[...]))
               * (math.sqrt(SB_HEAD_DIM) * 1.01))

    def live(carry):
        n, smin = carry
        return jnp.logical_and(n >= 0, z_bound - smin > -F32_MIN_NORMAL_LOG)

    def step(carry):
        n, _ = carry
        kvs = []
        for h, cs in enumerate(cols):
            if interleaved:
                rows = pl.ds(n * (tk * heads) + h, tk, stride=heads)
                kvs.append((ko_ref[rows, :], vo_ref[rows, :]))
            else:
                rows = pl.ds(pl.multiple_of(n * tk, tk), tk)
                kvs.append((ko_ref[rows, cs], vo_ref[rows, cs]))
        zos = [_dot_nt(q_ref[:, cs], kb) for cs, (kb, _) in zip(cols, kvs)]
        cos = [_suffix_sums(_softplus_logit(zo), tri_k) for zo in zos]
        smin = None
        for h, cs in enumerate(cols):
            suf = suf_ref[h]
            acc_ref[:, cs] += _dot(jnp.exp(zos[h] - cos[h] - suf), kvs[h][1])
            suf = suf + cos[h][:, 0:1]
            suf_ref[h] = suf
            hmin = jnp.min(suf)
            smin = hmin if smin is None else jnp.minimum(smin, hmin)
        return n - 1, smin

    lax.while_loop(live, step, (n_off - (2 if has_prev else 1), suf_min))
    o_ref[...] = acc_ref[...]


def _sb_attention(q, k_new, v_new, k_off, v_off, gq, gk, tq, tk, causal_prompt, layer=0):
    b, t, width = q.shape
    d = SB_HEAD_DIM
    heads = width // d
    blk = pl.BlockSpec((None, tq, width), lambda bi, i: (bi, i, 0))
    if causal_prompt:
        full = pl.BlockSpec((None,) + k_off.shape[1:], lambda bi, i: (bi, 0, 0), pipeline_mode=pl.Buffered(1))
    else:
        full = pl.BlockSpec((None, None) + k_off.shape[2:], lambda bi, i: (layer, bi, 0, 0),
                            pipeline_mode=pl.Buffered(1))
    in_specs = [blk, blk, blk]
    args = [q, k_new, v_new]
    if causal_prompt:
        assert tq == tk
        off_blocks = lambda i: i
        prev = pl.BlockSpec((None, tk, width), lambda bi, i: (bi, jnp.maximum(i - 1, 0), 0))
        in_specs += [prev, prev]
        args += [k_off, v_off]
    else:
        t_off = k_off.shape[2] // heads
        off_blocks = lambda i: t_off // tk
    in_specs += [full, full, _const_spec((1, d)), _const_spec((1, d))]
    args += [k_off, v_off, gq, gk]
    return pl.pallas_call(
        functools.partial(_sb_attn_kernel, heads=heads, tk=tk, off_blocks=off_blocks,
                          interleaved=not causal_prompt, has_prev=causal_prompt),
        out_shape=jax.ShapeDtypeStruct((b, t, width), F32),
        grid=(b, t // tq),
        in_specs=in_specs,
        out_specs=blk,
        scratch_shapes=[pltpu.VMEM((tq, width), F32), pltpu.VMEM((heads, tq, 1), F32)],
        compiler_params=_params("parallel", "arbitrary"),
        name="sb_attention",
    )(*args)


def _head_mean_matrix(width, head_dim):
    idx = jnp.arange(width) // head_dim
    return ((idx[:, None] == idx[None, :]).astype(F32) / head_dim).astype(BF16)


def kernel(x_prompt, x_sample, state_gdn_conv, state_gdn_s, cache_sb_k, cache_sb_v, cache_mem_k, cache_mem_v,
           mem_prompt, norm_mix, norm_mem, norm_ffn, w_in_gdn, w_in_sb, w_mem_kv, mem_q_gain, mem_k_gain,
           gdn_conv_w, gdn_a_log, gdn_dt_bias, gdn_o_gain, sb_q_gain, sb_k_gain, w_out_gdn, w_out_sb, w_up, w_down):
    depth, d_model = norm_mix.shape
    bp, t_p, _ = x_prompt.shape
    bs, t_s, _ = x_sample.shape
    gdn_heads = gdn_a_log.shape[1]
    gdn_width = gdn_heads * GDN_HEAD_DIM
    conv_ch = gdn_conv_w.shape[-1]
    n_sb, _, past, sb_heads, _ = cache_sb_k.shape
    sb_width = sb_heads * SB_HEAD_DIM
    mem_tokens, mem_heads = cache_mem_k.shape[2], cache_mem_k.shape[3]
    mem_width = mem_heads * MEM_HEAD_DIM
    assert MEM_HEAD_DIM & (MEM_HEAD_DIM - 1) == 0

    n_gate = 2 * gdn_heads
    tok_in = conv_ch + gdn_width + n_gate
    w_g = w_in_gdn
    w_gdn = jnp.concatenate(
        [w_g[..., :conv_ch + gdn_width],
         jnp.pad(w_g[..., conv_ch + gdn_width:tok_in], ((0, 0), (0, 0), (0, LANES - n_gate))),
         w_g[..., tok_in:]], axis=-1).astype(BF16)
    w_sb = w_in_sb.astype(BF16)
    w_kv = w_mem_kv.astype(BF16)
    w_og = w_out_gdn.astype(BF16)
    w_os = w_out_sb.astype(BF16)
    w_u = w_up.astype(BF16)
    w_d = w_down.astype(BF16)

    head_mean = _head_mean_matrix(mem_width, MEM_HEAD_DIM)
    q_gain_t = jnp.tile(mem_q_gain, (1, mem_heads))[:, None, :]
    k_gain_t = jnp.tile(mem_k_gain, (1, mem_heads))[:, None, :]
    lane_pad = lambda a: jnp.pad(a, ((0, 0), (gdn_heads, LANES - n_gate)))[:, None, :]
    neg_exp_alog = lane_pad(-jnp.exp(gdn_a_log))
    dt_bias = lane_pad(gdn_dt_bias)

    pmk, pmv = _mem_kv(mem_prompt, norm_mem[:, None, :], w_kv, k_gain_t, head_mean)
    pmk5 = pmk.reshape(depth, bp, mem_tokens, mem_heads, MEM_HEAD_DIM)
    pmv5 = pmv.reshape(depth, bp, mem_tokens, mem_heads, MEM_HEAD_DIM)
    smk = cache_mem_k.reshape(depth, bs, mem_tokens, mem_width)
    smv = cache_mem_v.reshape(depth, bs, mem_tokens, mem_width)
    cache_k = cache_sb_k.reshape(n_sb, bs, past * sb_heads, SB_HEAD_DIM)
    cache_v = cache_sb_v.reshape(n_sb, bs, past * sb_heads, SB_HEAD_DIM)

    yp, ys = x_prompt, x_sample
    pc, pst, sc, sst = [], [], [], []
    kv_stacks = {prompt: tuple(jnp.zeros((n_sb, rows * sb_heads, SB_HEAD_DIM), F32) for _ in range(2))
                 for prompt, rows in ((True, bp * t_p), (False, bs * t_s))}
    for i in range(depth):
        j = i // 2
        g_mix = norm_mix[i][None, :]
        paths = []
        for y, mk, mv, is_prompt in ((yp, pmk, pmv, True), (ys, smk, smv, False)):
            b, t, _ = y.shape
            y2 = y.reshape(b * t, d_model)
            if i % 2 == 0:
                if is_prompt:
                    buf0 = jnp.zeros((b, GDN_CONV - 1, conv_ch), F32)
                    st0 = jnp.zeros((b, gdn_heads, GDN_HEAD_DIM, GDN_HEAD_DIM), F32)
                else:
                    buf0, st0 = state_gdn_conv[j], state_gdn_s[j]
                o_mix, mq, nbuf, s_new = _gdn_mixer(y, g_mix, w_gdn, j, buf0, st0, gdn_conv_w[j],
                                                    neg_exp_alog[j], dt_bias[j], gdn_o_gain[j][None, :])
                (pc if is_prompt else sc).append(nbuf)
                (pst if is_prompt else sst).append(s_new)
                w_o = w_og
            else:
                gq, gk = sb_q_gain[j][None, :], sb_k_gain[j][None, :]
                q, kb, vb, mq, k_all, v_all = _in_proj_sb(y2, g_mix, w_sb, gq, gk, sb_heads, mem_width,
                                                          j, kv_stacks[is_prompt])
                kv_stacks[is_prompt] = (k_all, v_all)
                r3 = lambda a: a.reshape(b, t, sb_width)
                if is_prompt:
                    tq = min(t, SB_QUERY_BLOCK)
                    o_mix = _sb_attention(r3(q), r3(kb), r3(vb), r3(kb), r3(vb), gq, gk, tq, tq, True)
                else:
                    o_mix = _sb_attention(r3(q), r3(kb), r3(vb), cache_k, cache_v, gq, gk, t,
                                          min(past, SB_CACHE_BLOCK), False, j)
                w_o = w_os
            y = _mix_ffn(o_mix, mq.reshape(b, t, mem_width), mk, mv, i, q_gain_t, head_mean, w_o, j, y,
                         norm_ffn[:, None, :], w_u, w_d)
            paths.append(y)
        yp, ys = paths
    heads5 = lambda a, b, t: a.reshape(n_sb, b, t, sb_heads, SB_HEAD_DIM)
    pk, pv = (heads5(a, bp, t_p) for a in kv_stacks[True])
    sk, sv = (heads5(a, bs, t_s) for a in kv_stacks[False])
    return (yp, ys, jnp.stack(pc), jnp.stack(pst), pk, pv, pmk5, pmv5,
            jnp.stack(sc), jnp.stack(sst), sk, sv)
```

```python
import functools
import math

import jax
import jax.numpy as jnp
from jax import lax
from jax.experimental import pallas as pl
from jax.experimental.pallas import tpu as pltpu

F32 = jnp.float32
BF16 = jnp.bfloat16

NORM_EPS = 1e-6
CHUNK = 64
GDN_HEAD_DIM = 128
GDN_CONV = 4
GDN_BLOCK = 256
GDN_GROUP = 128
SB_HEAD_DIM = 128
SB_QUERY_BLOCK = 256
SB_CACHE_BLOCK = 256
MEM_HEAD_DIM = 64
LANES = 128
SUBLANES = 8
VMEM_LIMIT_BYTES = 56 * 1024 * 1024
ROW_TILE = 512
COL_CHUNK = 512
FF_CHUNK = 1024
F32_MIN_NORMAL_LOG = 88.0


def _params(*semantics):
    return pltpu.CompilerParams(dimension_semantics=semantics, vmem_limit_bytes=VMEM_LIMIT_BYTES)


def _dot(a, b):
    return jnp.dot(a.astype(BF16), b.astype(BF16), preferred_element_type=F32)


def _dot_nt(a, b):
    return lax.dot_general(a.astype(BF16), b.astype(BF16), (((1,), (1,)), ((), ())),
                           preferred_element_type=F32)


def _dot_tn(a, b):
    return lax.dot_general(a.astype(BF16), b.astype(BF16), (((0,), (0,)), ((), ())),
                           preferred_element_type=F32)


def _split_bf16(x, terms):
    parts = []
    for _ in range(terms):
        p = x.astype(BF16)
        parts.append(p)
        x = x - p.astype(F32)
    return parts


def _dot_exact_rhs(a, b_exact, terms):
    acc = None
    for p in _split_bf16(a, terms):
        t = jnp.dot(p, b_exact, preferred_element_type=F32)
        acc = t if acc is None else acc + t
    return acc


def _rms(x, gain):
    return x * lax.rsqrt(jnp.mean(x * x, axis=-1, keepdims=True) + NORM_EPS) * gain


def _sigmoid(x):
    return 1.0 / (1.0 + jnp.exp(-x))


def _softplus(x):
    return jnp.maximum(x, 0.0) + jnp.log1p(jnp.exp(-jnp.abs(x)))


def _const_spec(shape):
    zeros = (0,) * len(shape)
    return pl.BlockSpec(shape, lambda *_: zeros)


def _row_spec(tm, n):
    return pl.BlockSpec((tm, n), lambda i: (i, 0))


def _layer_spec(stacked, layer):
    zeros = (0,) * (stacked.ndim - 1)
    return pl.BlockSpec((None,) + stacked.shape[1:], lambda *_: (layer,) + zeros)


def _project(xb, w_ref, o_ref, off, n):
    for c in range(0, n, COL_CHUNK):
        cw = min(COL_CHUNK, n - c)
        o_ref[:, c:c + cw] = jnp.dot(xb, w_ref[:, off + c:off + c + cw],
                                     preferred_element_type=F32).astype(o_ref.dtype)


def _in_proj_sb_kernel(x_ref, g_ref, w_ref, gq_ref, gk_ref, *refs, heads, own, n_fill):
    q_ref, kb_ref, vb_ref, mq_ref, k_ref, v_ref = refs[-6:]
    xb = _rms(x_ref[...], g_ref[...]).astype(BF16)
    tm = x_ref.shape[0]
    d = SB_HEAD_DIM
    width = heads * d
    q_all = jnp.dot(xb, w_ref[:, :width], preferred_element_type=F32)
    k_all = jnp.dot(xb, w_ref[:, width:2 * width], preferred_element_type=F32)
    v_all = jnp.dot(xb, w_ref[:, 2 * width:3 * width], preferred_element_type=F32)
    vb_ref[...] = v_all.astype(BF16)
    for h in range(heads):
        cs = slice(h * d, (h + 1) * d)
        interleaved = pl.ds(h, tm, stride=heads)
        q_ref[:, cs] = (_rms(q_all[:, cs], gq_ref[...]) * (d ** -0.5)).astype(BF16)
        k = _rms(k_all[:, cs], gk_ref[...])
        k_ref[own, interleaved, :] = k
        kb_ref[:, cs] = k.astype(BF16)
        v_ref[own, interleaved, :] = v_all[:, cs]
    for s in range(n_fill):
        if s != own:
            k_ref[s] = jnp.zeros(k_ref.shape[1:], F32)
            v_ref[s] = jnp.zeros(v_ref.shape[1:], F32)
    _project(xb, w_ref, mq_ref, 3 * width, mq_ref.shape[-1])


def _in_proj_sb(x2, gain, w, gq, gk, heads, mem_width, slot, n_slots, kv_all):
    rows, d_model = x2.shape
    d = SB_HEAD_DIM
    width = heads * d
    tm = min(ROW_TILE, rows)
    stack_shape = jax.ShapeDtypeStruct((n_slots, rows * heads, d), F32)
    in_specs = [_row_spec(tm, d_model), _const_spec((1, d_model)), _layer_spec(w, slot),
                _const_spec((1, d)), _const_spec((1, d))]
    args = [x2, gain, w, gq, gk]
    if kv_all is None:
        stack_spec = pl.BlockSpec((n_slots, tm * heads, d), lambda i: (0, i, 0))
        own, n_fill, aliases = slot, n_slots, {}
    else:
        stack_spec = pl.BlockSpec((1, tm * heads, d), lambda i: (slot, i, 0))
        own, n_fill, aliases = 0, 0, {5: 4, 6: 5}
        in_specs += [pl.BlockSpec(memory_space=pl.ANY)] * 2
        args += list(kv_all)
    return pl.pallas_call(
        functools.partial(_in_proj_sb_kernel, heads=heads, own=own, n_fill=n_fill),
        out_shape=[jax.ShapeDtypeStruct((rows, width), BF16)] * 3
        + [jax.ShapeDtypeStruct((rows, mem_width), F32), stack_shape, stack_shape],
        grid=(rows // tm,),
        in_specs=in_specs,
        out_specs=[_row_spec(tm, width)] * 3 + [_row_spec(tm, mem_width), stack_spec, stack_spec],
        input_output_aliases=aliases,
        compiler_params=_params("parallel"),
        name="in_proj_sb",
    )(*args)


def _head_mean_sq(x, hm_ref):
    return _dot_exact_rhs(x * x, hm_ref[...], 2)


def _mem_kv_kernel(mem_ref, g_ref, w_ref, kg_ref, hm_ref, k_ref, v_ref):
    xb = _rms(mem_ref[...], g_ref[...]).astype(BF16)
    width = k_ref.shape[-1]
    k = jnp.dot(xb, w_ref[:, :width], preferred_element_type=F32)
    k_ref[...] = k * lax.rsqrt(_head_mean_sq(k, hm_ref) + NORM_EPS) * kg_ref[...]
    v_ref[...] = jnp.dot(xb, w_ref[:, width:], preferred_element_type=F32)


def _mem_kv(mem, norm_mem, w_kv, k_gain_t, head_mean):
    bp, m, d = mem.shape
    depth = w_kv.shape[0]
    width = w_kv.shape[-1] // 2
    return pl.pallas_call(
        _mem_kv_kernel,
        out_shape=[jax.ShapeDtypeStruct((depth, bp, m, width), F32)] * 2,
        grid=(depth, bp),
        in_specs=[pl.BlockSpec((None, m, d), lambda i, b: (b, 0, 0)),
                  pl.BlockSpec((None, 1, d), lambda i, b: (i, 0, 0)),
                  pl.BlockSpec((None, d, 2 * width), lambda i, b: (i, 0, 0)),
                  pl.BlockSpec((None, 1, width), lambda i, b: (i, 0, 0)),
                  _const_spec((width, width))],
        out_specs=[pl.BlockSpec((None, None, m, width), lambda i, b: (i, b, 0, 0))] * 2,
        compiler_params=_params("parallel", "parallel"),
        name="mem_kv",
    )(mem, norm_mem, w_kv, k_gain_t, head_mean)


def _mix_ffn_kernel(o_ref, mq_ref, k_ref, v_ref, qg_ref, hm_ref, wo_ref, x_ref, gf_ref, wu_ref, wd_ref,
                    y_ref, ymid_ref, mem_ref):
    q = mq_ref[...]
    mw = q.shape[-1]
    qn = q * lax.rsqrt(_head_mean_sq(q, hm_ref) + NORM_EPS) * qg_ref[...]
    lane_head = lax.broadcasted_iota(jnp.int32, (1, mw), 1) // MEM_HEAD_DIM
    n_batch = k_ref.shape[0]
    t = q.shape[0] // n_batch
    for bi in range(n_batch):
        rs = slice(bi * t, (bi + 1) * t)
        kb = k_ref[bi].astype(BF16)
        v = v_ref[bi]
        scores = [_dot_nt(jnp.where(lane_head == h, qn[rs], 0.0), kb) * (MEM_HEAD_DIM ** -0.5)
                  for h in range(mw // MEM_HEAD_DIM)]
        mem = None
        for h, sh in enumerate(scores):
            e = jnp.exp(sh - jnp.max(sh, axis=-1, keepdims=True))
            p = e / jnp.sum(e, axis=-1, keepdims=True)
            part = _dot(p, jnp.where(lane_head == h, v, 0.0))
            mem = part if mem is None else mem + part
        mem_ref[rs, :] = mem.astype(BF16)
    w_mix = o_ref.shape[-1]
    ymid_ref[...] = (x_ref[...] + _dot(o_ref[...], wo_ref[:w_mix, :])
                     + jnp.dot(mem_ref[...], wo_ref[w_mix:, :], preferred_element_type=F32))

    hb = _rms(ymid_ref[...], gf_ref[...]).astype(BF16)
    acc = None
    for c in range(0, wu_ref.shape[-1], FF_CHUNK):
        a = jnp.maximum(jnp.dot(hb, wu_ref[:, c:c + FF_CHUNK], preferred_element_type=F32), 0.0)
        t_c = jnp.dot((a * a).astype(BF16), wd_ref[c:c + FF_CHUNK, :], preferred_element_type=F32)
        acc = t_c if acc is None else acc + t_c
    y_ref[...] = ymid_ref[...] + acc


def _resident(spec_fn, *args):
    spec = spec_fn(*args)
    return pl.BlockSpec(spec.block_shape, spec.index_map, pipeline_mode=pl.Buffered(1))


def _mix_ffn(o_mix, mq, mk, mv, layer, q_gain_t, head_mean, w_out, w_layer, x, ffn_gain, w_up, w_down):
    b, t, d = x.shape
    rows = b * t
    w_mix = o_mix.shape[-1]
    mw = mq.shape[-1]
    tm = min(ROW_TILE, rows)
    assert t % tm == 0 or tm % t == 0
    n_batch = max(1, tm // t)
    tiles_per_seq = max(1, t // tm)
    flat = lambda a: a.reshape(rows, a.shape[-1])
    kv = pl.BlockSpec((None, n_batch) + mk.shape[2:], lambda i: (layer, i // tiles_per_seq, 0, 0))
    y = pl.pallas_call(
        _mix_ffn_kernel,
        out_shape=jax.ShapeDtypeStruct((rows, d), F32),
        grid=(rows // tm,),
        in_specs=[_row_spec(tm, w_mix), _row_spec(tm, mw), kv, kv, _layer_spec(q_gain_t, layer),
                  _const_spec((mw, mw)), _resident(_layer_spec, w_out, w_layer), _row_spec(tm, d),
                  _layer_spec(ffn_gain, layer), _resident(_layer_spec, w_up, layer),
                  _resident(_layer_spec, w_down, layer)],
        out_specs=_row_spec(tm, d),
        scratch_shapes=[pltpu.VMEM((tm, d), F32), pltpu.VMEM((tm, mw), BF16)],
        compiler_params=_params("parallel"),
        name="mix_ffn",
    )(flat(o_mix), flat(mq), mk, mv, q_gain_t, head_mean, w_out, flat(x), ffn_gain, w_up, w_down)
    return y.reshape(b, t, d)


CONV_PAD = SUBLANES


def _cat3(parts):
    hi, lo = parts
    return jnp.concatenate([hi, lo, hi], axis=1)


def _unit_lower_inverses(lowers, eye):
    r = eye.shape[0]
    ps = [-lower for lower in lowers]
    invs = [eye + p for p in ps]

    def split_rhs(p):
        parts = _split_bf16(p, 2)
        return parts, jnp.concatenate([parts[0], parts[0], parts[1]], axis=0)

    ps = [jnp.dot(_cat3(parts), rhs, preferred_element_type=F32) for parts, rhs in map(split_rhs, ps)]
    span = 4
    while span <= CHUNK:
        nxt_p, nxt_inv = [], []
        for p, inv in zip(ps, invs):
            parts, rhs = split_rhs(p)
            i_cat = _cat3(_split_bf16(inv, 2))
            if span < CHUNK:
                both = jnp.dot(jnp.concatenate([_cat3(parts), i_cat], axis=0), rhs, preferred_element_type=F32)
                nxt_p.append(both[:r])
                nxt_inv.append(inv + both[r:])
            else:
                nxt_inv.append(inv + jnp.dot(i_cat, rhs, preferred_element_type=F32))
        ps, invs = nxt_p, nxt_inv
        span *= 2

    def cat6_lhs(x):
        a0, a1, a2 = _split_bf16(x, 3)
        return jnp.concatenate([a0, a0, a0, a1, a1, a2], axis=1)

    def cat6_rhs(x):
        b0, b1, b2 = _split_bf16(x, 3)
        return jnp.concatenate([b0, b1, b2, b0, b1, b0], axis=0)

    resids = [eye - inv - jnp.dot(cat6_lhs(lower), cat6_rhs(inv), preferred_element_type=F32)
              for lower, inv in zip(lowers, invs)]
    return [inv + _dot(inv, resid) for inv, resid in zip(invs, resids)]


def _gdn_kernel(x_ref, gx_ref, w_ref, buf_ref, s0_ref, cw_ref, nal_ref, dt_ref, og_ref,
                o_ref, mq_ref, nbuf_ref, sout_ref,
                xp_ref, act_ref, z_ref, s_ref, gc_ref, kwqw_ref, ku_ref, au_ref, *, heads):
    blk = pl.program_id(1)
    tb = x_ref.shape[0]
    hd = GDN_HEAD_DIM
    width = heads * hd
    conv_ch = 3 * width
    tail = GDN_CONV - 1
    group = min(tb, GDN_GROUP)
    n_sub = group // CHUNK

    @pl.when(blk == 0)
    def _():
        xp_ref[CONV_PAD - tail:CONV_PAD, :] = buf_ref[...]
        s_ref[...] = s0_ref[...]

    xb = _rms(x_ref[...], gx_ref[...]).astype(BF16)

    def project(c0, c1):
        return jnp.dot(xb, w_ref[:, c0:c1], preferred_element_type=F32)

    for c0 in range(0, conv_ch, COL_CHUNK):
        c1 = min(c0 + COL_CHUNK, conv_ch)
        xp_ref[CONV_PAD:CONV_PAD + tb, c0:c1] = project(c0, c1)
        for ct in range(c0 // hd, c1 // hd):
            cs = slice(ct * hd, (ct + 1) * hd)
            y = None
            for j in range(GDN_CONV):
                r0 = CONV_PAD - tail + j
                term = xp_ref[r0:r0 + tb, cs] * cw_ref[j:j + 1, cs]
                y = term if y is None else y + term
            y = y * _sigmoid(y)
            if ct < 2 * heads:
                y = y * lax.rsqrt(jnp.sum(y * y, axis=-1, keepdims=True) + NORM_EPS)
                if ct < heads:
                    y = y * (hd ** -0.5)
            act_ref[:, cs] = y
    for c0 in range(0, width, COL_CHUNK):
        c1 = min(c0 + COL_CHUNK, width)
        z_ref[:, c0:c1] = project(conv_ch + c0, conv_ch + c1)
    ba = project(conv_ch + width, conv_ch + width + LANES)
    mq_ref[...] = project(conv_ch + width + LANES, w_ref.shape[-1])

    new_tail = xp_ref[CONV_PAD + tb - tail:CONV_PAD + tb, :]
    xp_ref[CONV_PAD - tail:CONV_PAD, :] = new_tail

    beta_all = _sigmoid(ba)
    g_all = nal_ref[...] * _softplus(ba + dt_ref[...])

    row = lax.broadcasted_iota(jnp.int32, (group, group), 0)
    col = lax.broadcasted_iota(jnp.int32, (group, group), 1)
    same = (row // CHUNK) == (col // CHUNK)
    causal = jnp.logical_and(same, row >= col)
    strict = jnp.logical_and(same, row > col)
    tri3 = jnp.concatenate([causal.astype(BF16)] * 3, axis=1)
    upper_f = jnp.logical_and(same, row <= col).astype(F32)
    eye = (row == col).astype(F32)
    sub_of_row = lax.broadcasted_iota(jnp.int32, (group, hd), 0) // CHUNK

    pairs = [(gi, h) for gi in range(tb // group) for h in range(heads)]
    gcs = []
    for gi in range(tb // group):
        rs = slice(gi * group, (gi + 1) * group)
        gc = jnp.dot(tri3, jnp.concatenate(_split_bf16(g_all[rs, :], 3), axis=0), preferred_element_type=F32)
        gc_ref[rs, :] = gc
        gcs.append(gc)

    def head_slices(gi, h):
        rs = slice(gi * group, (gi + 1) * group)
        return (act_ref[rs, h * hd:(h + 1) * hd], act_ref[rs, width + h * hd:width + (h + 1) * hd],
                act_ref[rs, 2 * width + h * hd:2 * width + (h + 1) * hd])

    def gate_columns(gi, h):
        gcol = gcs[gi][:, heads + h:heads + h + 1]
        bcol = beta_all[gi * group:(gi + 1) * group, h:h + 1]
        return gcol, bcol

    lowers, a_intras = [], []
    for gi, h in pairs:
        rs = slice(gi * group, (gi + 1) * group)
        gcol, bcol = gate_columns(gi, h)
        grow = jnp.sum(g_all[rs, heads + h:heads + h + 1] * upper_f, axis=0, keepdims=True)
        decay = jnp.where(causal, jnp.exp(jnp.where(causal, gcol - grow, 0.0)), 0.0)
        q, k, _ = head_slices(gi, h)
        kq = _dot_nt(jnp.concatenate([k * bcol, q], axis=0), k)
        lowers.append(kq[:group] * jnp.where(strict, decay, 0.0))
        a_intras.append(kq[group:] * decay)

    invs = _unit_lower_inverses(lowers, eye)

    sols = []
    for (gi, h), inv in zip(pairs, invs):
        gcol, bcol = gate_columns(gi, h)
        _, k, v = head_slices(gi, h)
        sols.append(_dot(inv, jnp.concatenate([v * bcol, k * bcol * jnp.exp(gcol)], axis=-1)).astype(BF16))

    for (gi, h), sol, a_intra in zip(pairs, sols, a_intras):
        gcol, _ = gate_columns(gi, h)
        glast = jnp.concatenate(
            [jnp.broadcast_to(gcol[(s + 1) * CHUNK - 1:(s + 1) * CHUNK, :], (CHUNK, 1)) for s in range(n_sub)],
            axis=0)
        q, k, _ = head_slices(gi, h)
        k_dec = k * jnp.exp(glast - gcol)
        if n_sub > 1:
            k_dec = jnp.concatenate([jnp.where(sub_of_row == s, k_dec, 0.0) for s in range(n_sub)], axis=1)
        kd_sol = _dot_tn(k_dec, sol)
        a_sol = _dot(a_intra, sol)
        qw = q * jnp.exp(gcol) - a_sol[:, hd:]
        for s in range(n_sub):
            idx = (gi * n_sub + s) * heads + h
            ku_ref[idx] = kd_sol[s * hd:(s + 1) * hd, :hd]
            au_ref[idx] = a_sol[s * CHUNK:(s + 1) * CHUNK, :hd]
            kwqw_ref[idx, :hd, :] = kd_sol[s * hd:(s + 1) * hd, hd:].astype(BF16)
            kwqw_ref[idx, hd:, :] = qw[s * CHUNK:(s + 1) * CHUNK, :].astype(BF16)

    for c in range(tb // CHUNK):
        rs = slice(c * CHUNK, (c + 1) * CHUNK)
        for h in range(heads):
            idx = c * heads + h
            gl = heads + h
            s = s_ref[h]
            glast = gc_ref[(c + 1) * CHUNK - 1:(c + 1) * CHUNK, gl:gl + 1]
            r = jnp.dot(kwqw_ref[idx], s.astype(BF16), preferred_element_type=F32)
            s_ref[h] = s * jnp.exp(glast) - r[:hd] + ku_ref[idx]
            o = r[hd:] + au_ref[idx]
            zh = z_ref[rs, h * hd:(h + 1) * hd]
            o_ref[rs, h * hd:(h + 1) * hd] = (_rms(o, og_ref[...]) * (zh * _sigmoid(zh))).astype(o_ref.dtype)

    @pl.when(blk == pl.num_programs(1) - 1)
    def _():
        nbuf_ref[...] = new_tail
        sout_ref[...] = s_ref[...]


def _gdn_mixer(x, gain, w_in, layer, conv_buf, s0, conv_w, neg_exp_alog, dt_bias, o_gain):
    b, t, d_model = x.shape
    heads = s0.shape[1]
    hd = GDN_HEAD_DIM
    width = heads * hd
    cw = 3 * width
    mem_width = w_in.shape[-1] - cw - width - LANES
    tail = GDN_CONV - 1
    tb = min(t, GDN_BLOCK)
    n_ch = (tb // CHUNK) * heads
    tok = lambda n: pl.BlockSpec((None, tb, n), lambda bi, i: (bi, i, 0))
    return pl.pallas_call(
        functools.partial(_gdn_kernel, heads=heads),
        out_shape=[jax.ShapeDtypeStruct((b, t, width), BF16),
                   jax.ShapeDtypeStruct((b, t, mem_width), F32),
                   jax.ShapeDtypeStruct((b, tail, cw), F32),
                   jax.ShapeDtypeStruct((b, heads, hd, hd), F32)],
        grid=(b, t // tb),
        in_specs=[tok(d_model), _const_spec((1, d_model)), _layer_spec(w_in, layer),
                  pl.BlockSpec((None, tail, cw), lambda bi, i: (bi, 0, 0)),
                  pl.BlockSpec((None, heads, hd, hd), lambda bi, i: (bi, 0, 0, 0)),
                  _const_spec((GDN_CONV, cw)), _const_spec((1, LANES)), _const_spec((1, LANES)),
                  _const_spec((1, hd))],
        out_specs=[tok(width), tok(mem_width),
                   pl.BlockSpec((None, tail, cw), lambda bi, i: (bi, 0, 0)),
                   pl.BlockSpec((None, heads, hd, hd), lambda bi, i: (bi, 0, 0, 0))],
        scratch_shapes=[pltpu.VMEM((CONV_PAD + tb, cw), F32), pltpu.VMEM((tb, cw), F32),
                        pltpu.VMEM((tb, width), F32),
                        pltpu.VMEM((heads, hd, hd), F32), pltpu.VMEM((tb, LANES), F32),
                        pltpu.VMEM((n_ch, hd + CHUNK, hd), BF16), pltpu.VMEM((n_ch, hd, hd), F32),
                        pltpu.VMEM((n_ch, CHUNK, hd), F32)],
        compiler_params=_params("parallel", "arbitrary"),
        name="gdn_mixer",
    )(x, gain, w_in, conv_buf, s0, conv_w, neg_exp_alog, dt_bias, o_gain)


def _softplus_logit(z):
    return jnp.log(1.0 + jnp.exp(z))


def _suffix_sums(m, tri2):
    return jnp.dot(jnp.concatenate(_split_bf16(m, 2), axis=1), tri2, preferred_element_type=F32)


def _tri2(n):
    row = lax.broadcasted_iota(jnp.int32, (n, n), 0)
    col = lax.broadcasted_iota(jnp.int32, (n, n), 1)
    tri = (row >= col).astype(BF16)
    return jnp.concatenate([tri, tri], axis=0), row > col


def _sb_attn_kernel(q_ref, kd_ref, vd_ref, kp_ref, vp_ref, k_hbm, v_hbm, gq_ref, gk_ref, o_ref,
                    acc_ref, suf_ref, kbuf_ref, vbuf_ref, *, heads, tk, n_off_blocks, interleaved, hbm_block):
    tq = q_ref.shape[0]
    d = SB_HEAD_DIM
    n_off = n_off_blocks(pl.program_id(1))
    cols = [slice(h * d, (h + 1) * d) for h in range(heads)]

    def head_rows(ref, h):
        if interleaved:
            return ref[pl.ds(h, tk, stride=heads), :]
        return ref[:, cols[h]]

    tri_q, strict = _tri2(tq)
    tri_k = tri_q if tk == tq else _tri2(tk)[0]
    zs = [_dot_nt(q_ref[:, cs], kd_ref[:, cs]) for cs in cols]
    cds = [_suffix_sums(jnp.where(strict, _softplus_logit(z), 0.0), tri_q) for z in zs]
    sufs = [c[:, 0:1] for c in cds]
    weights = [jnp.where(strict, jnp.exp(z - c), 0.0) for z, c in zip(zs, cds)]
    valid = n_off > 0
    zps = [_dot_nt(q_ref[:, cs], head_rows(kp_ref, h)) for h, cs in enumerate(cols)]
    cps = [_suffix_sums(_softplus_logit(zp), tri_k) for zp in zps]
    weights_p = [jnp.where(valid, jnp.exp(zp - cp - suf), 0.0) for zp, cp, suf in zip(zps, cps, sufs)]
    sufs = [suf + jnp.where(valid, cp[:, 0:1], 0.0) for suf, cp in zip(sufs, cps)]
    suf_min = None
    for h, cs in enumerate(cols):
        acc_ref[:, cs] = _dot(weights[h], vd_ref[:, cs]) + _dot(weights_p[h], head_rows(vp_ref, h))
        suf_ref[h] = sufs[h]
        hmin = jnp.min(sufs[h])
        suf_min = hmin if suf_min is None else jnp.minimum(suf_min, hmin)

    z_bound = (jnp.max(jnp.abs(gq_ref[...])) * jnp.max(jnp.abs(gk_ref[...]))
               * (math.sqrt(SB_HEAD_DIM) * 1.01))

    def live(carry):
        n, smin = carry
        return jnp.logical_and(n >= 0, z_bound - smin > -F32_MIN_NORMAL_LOG)

    def step(carry):
        n, _ = carry
        pltpu.sync_copy(hbm_block(k_hbm, n), kbuf_ref)
        pltpu.sync_copy(hbm_block(v_hbm, n), vbuf_ref)
        zos = [_dot_nt(q_ref[:, cs], head_rows(kbuf_ref, h)) for h, cs in enumerate(cols)]
        cos = [_suffix_sums(_softplus_logit(zo), tri_k) for zo in zos]
        smin = None
        for h, cs in enumerate(cols):
            suf = suf_ref[h]
            acc_ref[:, cs] += _dot(jnp.exp(zos[h] - cos[h] - suf), head_rows(vbuf_ref, h))
            suf = suf + cos[h][:, 0:1]
            suf_ref[h] = suf
            hmin = jnp.min(suf)
            smin = hmin if smin is None else jnp.minimum(smin, hmin)
        return n - 1, smin

    lax.while_loop(live, step, (n_off - 2, suf_min))
    o_ref[...] = acc_ref[...].astype(o_ref.dtype)


def _sb_attention(q, k_new, v_new, k_off, v_off, gq, gk, tq, tk, causal_prompt, layer=0):
    b, t, width = q.shape
    d = SB_HEAD_DIM
    heads = width // d
    blk = pl.BlockSpec((None, tq, width), lambda bi, i: (bi, i, 0))
    if causal_prompt:
        assert tq == tk
        n_off_blocks = lambda i: i
        prev = pl.BlockSpec((None, tk, width), lambda bi, i: (bi, jnp.maximum(i - 1, 0), 0))
        hbm_block = lambda ref, n: ref.at[pl.program_id(0), pl.ds(pl.multiple_of(n * tk, tk), tk), :]
        buf = pltpu.VMEM((tk, width), k_off.dtype)
    else:
        t_off = k_off.shape[2] // heads
        assert t_off % tk == 0 and t_off >= tk
        n_blocks = t_off // tk
        n_off_blocks = lambda i: n_blocks
        rows = tk * heads
        prev = pl.BlockSpec((None, None, rows, d), lambda bi, i: (layer, bi, n_blocks - 1, 0))
        hbm_block = lambda ref, n: ref.at[layer, pl.program_id(0), pl.ds(pl.multiple_of(n * rows, rows), rows), :]
        buf = pltpu.VMEM((rows, d), k_off.dtype)
    hbm = pl.BlockSpec(memory_space=pl.ANY)
    return pl.pallas_call(
        functools.partial(_sb_attn_kernel, heads=heads, tk=tk, n_off_blocks=n_off_blocks,
                          interleaved=not causal_prompt, hbm_block=hbm_block),
        out_shape=jax.ShapeDtypeStruct((b, t, width), BF16),
        grid=(b, t // tq),
        in_specs=[blk, blk, blk, prev, prev, hbm, hbm, _const_spec((1, d)), _const_spec((1, d))],
        out_specs=blk,
        scratch_shapes=[pltpu.VMEM((tq, width), F32), pltpu.VMEM((heads, tq, 1), F32), buf, buf],
        compiler_params=_params("parallel", "arbitrary"),
        name="sb_attention",
    )(q, k_new, v_new, k_off, v_off, k_off, v_off, gq, gk)


def _head_mean_matrix(width, head_dim):
    idx = jnp.arange(width) // head_dim
    return ((idx[:, None] == idx[None, :]).astype(F32) / head_dim).astype(BF16)


def kernel(x_prompt, x_sample, state_gdn_conv, state_gdn_s, cache_sb_k, cache_sb_v, cache_mem_k, cache_mem_v,
           mem_prompt, norm_mix, norm_mem, norm_ffn, w_in_gdn, w_in_sb, w_mem_kv, mem_q_gain, mem_k_gain,
           gdn_conv_w, gdn_a_log, gdn_dt_bias, gdn_o_gain, sb_q_gain, sb_k_gain, w_out_gdn, w_out_sb, w_up, w_down):
    depth, d_model = norm_mix.shape
    bp, t_p, _ = x_prompt.shape
    bs, t_s, _ = x_sample.shape
    gdn_heads = gdn_a_log.shape[1]
    gdn_width = gdn_heads * GDN_HEAD_DIM
    conv_ch = gdn_conv_w.shape[-1]
    n_sb, _, past, sb_heads, _ = cache_sb_k.shape
    sb_width = sb_heads * SB_HEAD_DIM
    mem_tokens, mem_heads = cache_mem_k.shape[2], cache_mem_k.shape[3]
    mem_width = mem_heads * MEM_HEAD_DIM
    assert MEM_HEAD_DIM & (MEM_HEAD_DIM - 1) == 0

    n_gate = 2 * gdn_heads
    tok_in = conv_ch + gdn_width + n_gate
    w_g = w_in_gdn.astype(BF16)
    w_gdn = jnp.concatenate(
        [w_g[..., :conv_ch + gdn_width],
         jnp.pad(w_g[..., conv_ch + gdn_width:tok_in], ((0, 0), (0, 0), (0, LANES - n_gate))),
         w_g[..., tok_in:]], axis=-1)
    w_sb = w_in_sb.astype(BF16)
    w_kv = w_mem_kv.astype(BF16)
    w_og = w_out_gdn.astype(BF16)
    w_os = w_out_sb.astype(BF16)
    w_u = w_up.astype(BF16)
    w_d = w_down.astype(BF16)

    head_mean = _head_mean_matrix(mem_width, MEM_HEAD_DIM)
    q_gain_t = jnp.tile(mem_q_gain, (1, mem_heads))[:, None, :]
    k_gain_t = jnp.tile(mem_k_gain, (1, mem_heads))[:, None, :]
    lane_pad = lambda a: jnp.pad(a, ((0, 0), (gdn_heads, LANES - n_gate)))[:, None, :]
    neg_exp_alog = lane_pad(-jnp.exp(gdn_a_log))
    dt_bias = lane_pad(gdn_dt_bias)

    pmk, pmv = _mem_kv(mem_prompt, norm_mem[:, None, :], w_kv, k_gain_t, head_mean)
    pmk5 = pmk.reshape(depth, bp, mem_tokens, mem_heads, MEM_HEAD_DIM)
    pmv5 = pmv.reshape(depth, bp, mem_tokens, mem_heads, MEM_HEAD_DIM)
    smk = cache_mem_k.reshape(depth, bs, mem_tokens, mem_width)
    smv = cache_mem_v.reshape(depth, bs, mem_tokens, mem_width)
    cache_k = cache_sb_k.reshape(n_sb, bs, past * sb_heads, SB_HEAD_DIM)
    cache_v = cache_sb_v.reshape(n_sb, bs, past * sb_heads, SB_HEAD_DIM)

    yp, ys = x_prompt, x_sample
    pc, pst, sc, sst = [], [], [], []
    kv_stacks = {True: None, False: None}
    for i in range(depth):
        j = i // 2
        g_mix = norm_mix[i][None, :]
        paths = []
        for y, mk, mv, is_prompt in ((yp, pmk, pmv, True), (ys, smk, smv, False)):
            b, t, _ = y.shape
            y2 = y.reshape(b * t, d_model)
            if i % 2 == 0:
                if is_prompt:
                    buf0 = jnp.zeros((b, GDN_CONV - 1, conv_ch), F32)
                    st0 = jnp.zeros((b, gdn_heads, GDN_HEAD_DIM, GDN_HEAD_DIM), F32)
                else:
                    buf0, st0 = state_gdn_conv[j], state_gdn_s[j]
                o_mix, mq, nbuf, s_new = _gdn_mixer(y, g_mix, w_gdn, j, buf0, st0, gdn_conv_w[j],
                                                    neg_exp_alog[j], dt_bias[j], gdn_o_gain[j][None, :])
                (pc if is_prompt else sc).append(nbuf)
                (pst if is_prompt else sst).append(s_new)
                w_o = w_og
            else:
                gq, gk = sb_q_gain[j][None, :], sb_k_gain[j][None, :]
                q, kb, vb, mq, k_all, v_all = _in_proj_sb(y2, g_mix, w_sb, gq, gk, sb_heads, mem_width,
                                                          j, n_sb, kv_stacks[is_prompt])
                kv_stacks[is_prompt] = (k_all, v_all)
                r3 = lambda a: a.reshape(b, t, sb_width)
                if is_prompt:
                    tq = min(t, SB_QUERY_BLOCK)
                    o_mix = _sb_attention(r3(q), r3(kb), r3(vb), r3(kb), r3(vb), gq, gk, tq, tq, True)
                else:
                    o_mix = _sb_attention(r3(q), r3(kb), r3(vb), cache_k, cache_v, gq, gk, t,
                                          min(past, SB_CACHE_BLOCK), False, j)
                w_o = w_os
            y = _mix_ffn(o_mix, mq.reshape(b, t, mem_width), mk, mv, i, q_gain_t, head_mean, w_o, j, y,
                         norm_ffn[:, None, :], w_u, w_d)
            paths.append(y)
        yp, ys = paths
    heads5 = lambda a, b, t: a.reshape(n_sb, b, t, sb_heads, SB_HEAD_DIM)
    pk, pv = (heads5(a, bp, t_p) for a in kv_stacks[True])
    sk, sv = (heads5(a, bs, t_s) for a in kv_stacks[False])
    return (yp, ys, jnp.stack(pc), jnp.stack(pst), pk, pv, pmk5, pmv5,
            jnp.stack(sc), jnp.stack(sst), sk, sv)
```

```python
import functools
import math

import jax
import jax.numpy as jnp
from jax import lax
from jax.experimental import pallas as pl
from jax.experimental.pallas import tpu as pltpu

F32 = jnp.float32
BF16 = jnp.bfloat16

NORM_EPS = 1e-6
CHUNK = 64
GDN_HEAD_DIM = 128
GDN_CONV = 4
GDN_BLOCK = 256
GDN_GROUP = 128
SB_HEAD_DIM = 128
SB_QUERY_BLOCK = 256
SB_CACHE_BLOCK = 256
MEM_HEAD_DIM = 64
LANES = 128
SUBLANES = 8
VMEM_LIMIT_BYTES = 56 * 1024 * 1024
ROW_TILE = 512
COL_CHUNK = 512
FF_CHUNK = 1024
F32_MIN_NORMAL_LOG = 88.0


def _params(*semantics):
    return pltpu.CompilerParams(dimension_semantics=semantics, vmem_limit_bytes=VMEM_LIMIT_BYTES)


def _dot(a, b):
    return jnp.dot(a.astype(BF16), b.astype(BF16), preferred_element_type=F32)


def _dot_nt(a, b):
    return lax.dot_general(a.astype(BF16), b.astype(BF16), (((1,), (1,)), ((), ())),
                           preferred_element_type=F32)


def _dot_tn(a, b):
    return lax.dot_general(a.astype(BF16), b.astype(BF16), (((0,), (0,)), ((), ())),
                           preferred_element_type=F32)


def _split_bf16(x, terms):
    parts = []
    for _ in range(terms):
        p = x.astype(BF16)
        parts.append(p)
        x = x - p.astype(F32)
    return parts


def _dot_exact_rhs(a, b_exact, terms):
    acc = None
    for p in _split_bf16(a, terms):
        t = jnp.dot(p, b_exact, preferred_element_type=F32)
        acc = t if acc is None else acc + t
    return acc


def _rms(x, gain):
    return x * lax.rsqrt(jnp.mean(x * x, axis=-1, keepdims=True) + NORM_EPS) * gain


def _sigmoid(x):
    return 1.0 / (1.0 + jnp.exp(-x))


def _softplus(x):
    return jnp.maximum(x, 0.0) + jnp.log1p(jnp.exp(-jnp.abs(x)))


def _const_spec(shape):
    zeros = (0,) * len(shape)
    return pl.BlockSpec(shape, lambda *_: zeros)


def _row_spec(tm, n):
    return pl.BlockSpec((tm, n), lambda i: (i, 0))


def _layer_spec(stacked, layer):
    zeros = (0,) * (stacked.ndim - 1)
    return pl.BlockSpec((None,) + stacked.shape[1:], lambda *_: (layer,) + zeros)


def _project(xb, w_ref, o_ref, off, n):
    for c in range(0, n, COL_CHUNK):
        cw = min(COL_CHUNK, n - c)
        o_ref[:, c:c + cw] = jnp.dot(xb, w_ref[:, off + c:off + c + cw],
                                     preferred_element_type=F32).astype(o_ref.dtype)


def _in_proj_sb_kernel(x_ref, g_ref, w_ref, gq_ref, gk_ref, *refs, heads, own, n_fill):
    q_ref, kb_ref, vb_ref, mq_ref, k_ref, v_ref = refs[-6:]
    xb = _rms(x_ref[...], g_ref[...]).astype(BF16)
    tm = x_ref.shape[0]
    d = SB_HEAD_DIM
    width = heads * d
    q_all = jnp.dot(xb, w_ref[:, :width], preferred_element_type=F32)
    k_all = jnp.dot(xb, w_ref[:, width:2 * width], preferred_element_type=F32)
    v_all = jnp.dot(xb, w_ref[:, 2 * width:3 * width], preferred_element_type=F32)
    vb_ref[...] = v_all.astype(BF16)
    for h in range(heads):
        cs = slice(h * d, (h + 1) * d)
        interleaved = pl.ds(h, tm, stride=heads)
        q_ref[:, cs] = (_rms(q_all[:, cs], gq_ref[...]) * (d ** -0.5)).astype(BF16)
        k = _rms(k_all[:, cs], gk_ref[...])
        k_ref[own, interleaved, :] = k
        kb_ref[:, cs] = k.astype(BF16)
        v_ref[own, interleaved, :] = v_all[:, cs]
    for s in range(n_fill):
        if s != own:
            k_ref[s] = jnp.zeros(k_ref.shape[1:], F32)
            v_ref[s] = jnp.zeros(v_ref.shape[1:], F32)
    _project(xb, w_ref, mq_ref, 3 * width, mq_ref.shape[-1])


def _in_proj_sb(x2, gain, w, gq, gk, heads, mem_width, slot, n_slots, kv_all):
    rows, d_model = x2.shape
    d = SB_HEAD_DIM
    width = heads * d
    tm = min(ROW_TILE, rows)
    stack_shape = jax.ShapeDtypeStruct((n_slots, rows * heads, d), F32)
    in_specs = [_row_spec(tm, d_model), _const_spec((1, d_model)), _layer_spec(w, slot),
                _const_spec((1, d)), _const_spec((1, d))]
    args = [x2, gain, w, gq, gk]
    if kv_all is None:
        stack_spec = pl.BlockSpec((n_slots, tm * heads, d), lambda i: (0, i, 0))
        own, n_fill, aliases = slot, n_slots, {}
    else:
        stack_spec = pl.BlockSpec((1, tm * heads, d), lambda i: (slot, i, 0))
        own, n_fill, aliases = 0, 0, {5: 4, 6: 5}
        in_specs += [pl.BlockSpec(memory_space=pl.ANY)] * 2
        args += list(kv_all)
    return pl.pallas_call(
        functools.partial(_in_proj_sb_kernel, heads=heads, own=own, n_fill=n_fill),
        out_shape=[jax.ShapeDtypeStruct((rows, width), BF16)] * 3
        + [jax.ShapeDtypeStruct((rows, mem_width), F32), stack_shape, stack_shape],
        grid=(rows // tm,),
        in_specs=in_specs,
        out_specs=[_row_spec(tm, width)] * 3 + [_row_spec(tm, mem_width), stack_spec, stack_spec],
        input_output_aliases=aliases,
        compiler_params=_params("parallel"),
        name="in_proj_sb",
    )(*args)


def _head_mean_sq(x, hm_ref):
    return _dot_exact_rhs(x * x, hm_ref[...], 2)


def _mem_kv_kernel(mem_ref, g_ref, w_ref, kg_ref, hm_ref, k_ref, v_ref):
    xb = _rms(mem_ref[...], g_ref[...]).astype(BF16)
    width = k_ref.shape[-1]
    k = jnp.dot(xb, w_ref[:, :width], preferred_element_type=F32)
    k_ref[...] = k * lax.rsqrt(_head_mean_sq(k, hm_ref) + NORM_EPS) * kg_ref[...]
    v_ref[...] = jnp.dot(xb, w_ref[:, width:], preferred_element_type=F32)


def _mem_kv(mem, norm_mem, w_kv, k_gain_t, head_mean):
    bp, m, d = mem.shape
    depth = w_kv.shape[0]
    width = w_kv.shape[-1] // 2
    return pl.pallas_call(
        _mem_kv_kernel,
        out_shape=[jax.ShapeDtypeStruct((depth, bp, m, width), F32)] * 2,
        grid=(depth, bp),
        in_specs=[pl.BlockSpec((None, m, d), lambda i, b: (b, 0, 0)),
                  pl.BlockSpec((None, 1, d), lambda i, b: (i, 0, 0)),
                  pl.BlockSpec((None, d, 2 * width), lambda i, b: (i, 0, 0)),
                  pl.BlockSpec((None, 1, width), lambda i, b: (i, 0, 0)),
                  _const_spec((width, width))],
        out_specs=[pl.BlockSpec((None, None, m, width), lambda i, b: (i, b, 0, 0))] * 2,
        compiler_params=_params("parallel", "parallel"),
        name="mem_kv",
    )(mem, norm_mem, w_kv, k_gain_t, head_mean)


def _mix_ffn_kernel(o_ref, mq_ref, k_ref, v_ref, qg_ref, hm_ref, wo_ref, x_ref, gf_ref, wu_ref, wd_ref,
                    y_ref, ymid_ref, mem_ref):
    q = mq_ref[...]
    mw = q.shape[-1]
    qn = q * lax.rsqrt(_head_mean_sq(q, hm_ref) + NORM_EPS) * qg_ref[...]
    lane_head = lax.broadcasted_iota(jnp.int32, (1, mw), 1) // MEM_HEAD_DIM
    n_batch = k_ref.shape[0]
    t = q.shape[0] // n_batch
    for bi in range(n_batch):
        rs = slice(bi * t, (bi + 1) * t)
        kb = k_ref[bi].astype(BF16)
        v = v_ref[bi]
        scores = [_dot_nt(jnp.where(lane_head == h, qn[rs], 0.0), kb) * (MEM_HEAD_DIM ** -0.5)
                  for h in range(mw // MEM_HEAD_DIM)]
        mem = None
        for h, sh in enumerate(scores):
            e = jnp.exp(sh - jnp.max(sh, axis=-1, keepdims=True))
            p = e / jnp.sum(e, axis=-1, keepdims=True)
            part = _dot(p, jnp.where(lane_head == h, v, 0.0))
            mem = part if mem is None else mem + part
        mem_ref[rs, :] = mem.astype(BF16)
    w_mix = o_ref.shape[-1]
    ymid_ref[...] = (x_ref[...] + _dot(o_ref[...], wo_ref[:w_mix, :])
                     + jnp.dot(mem_ref[...], wo_ref[w_mix:, :], preferred_element_type=F32))

    hb = _rms(ymid_ref[...], gf_ref[...]).astype(BF16)
    acc = None
    for c in range(0, wu_ref.shape[-1], FF_CHUNK):
        a = jnp.maximum(jnp.dot(hb, wu_ref[:, c:c + FF_CHUNK], preferred_element_type=F32), 0.0)
        t_c = jnp.dot((a * a).astype(BF16), wd_ref[c:c + FF_CHUNK, :], preferred_element_type=F32)
        acc = t_c if acc is None else acc + t_c
    y_ref[...] = ymid_ref[...] + acc


def _resident(spec_fn, *args):
    spec = spec_fn(*args)
    return pl.BlockSpec(spec.block_shape, spec.index_map, pipeline_mode=pl.Buffered(1))


def _mix_ffn(o_mix, mq, mk, mv, layer, q_gain_t, head_mean, w_out, w_layer, x, ffn_gain, w_up, w_down):
    b, t, d = x.shape
    rows = b * t
    w_mix = o_mix.shape[-1]
    mw = mq.shape[-1]
    tm = min(ROW_TILE, rows)
    assert t % tm == 0 or tm % t == 0
    n_batch = max(1, tm // t)
    tiles_per_seq = max(1, t // tm)
    flat = lambda a: a.reshape(rows, a.shape[-1])
    kv = pl.BlockSpec((None, n_batch) + mk.shape[2:], lambda i: (layer, i // tiles_per_seq, 0, 0))
    y = pl.pallas_call(
        _mix_ffn_kernel,
        out_shape=jax.ShapeDtypeStruct((rows, d), F32),
        grid=(rows // tm,),
        in_specs=[_row_spec(tm, w_mix), _row_spec(tm, mw), kv, kv, _layer_spec(q_gain_t, layer),
                  _const_spec((mw, mw)), _resident(_layer_spec, w_out, w_layer), _row_spec(tm, d),
                  _layer_spec(ffn_gain, layer), _resident(_layer_spec, w_up, layer),
                  _resident(_layer_spec, w_down, layer)],
        out_specs=_row_spec(tm, d),
        scratch_shapes=[pltpu.VMEM((tm, d), F32), pltpu.VMEM((tm, mw), BF16)],
        compiler_params=_params("parallel"),
        name="mix_ffn",
    )(flat(o_mix), flat(mq), mk, mv, q_gain_t, head_mean, w_out, flat(x), ffn_gain, w_up, w_down)
    return y.reshape(b, t, d)


CONV_PAD = SUBLANES


def _cat3(parts):
    hi, lo = parts
    return jnp.concatenate([hi, lo, hi], axis=1)


def _unit_lower_inverses(lowers, eye):
    r = eye.shape[0]
    ps = [-lower for lower in lowers]
    invs = [eye + p for p in ps]

    def split_rhs(p):
        parts = _split_bf16(p, 2)
        return parts, jnp.concatenate([parts[0], parts[0], parts[1]], axis=0)

    ps = [jnp.dot(_cat3(parts), rhs, preferred_element_type=F32) for parts, rhs in map(split_rhs, ps)]
    span = 4
    while span <= CHUNK:
        nxt_p, nxt_inv = [], []
        for p, inv in zip(ps, invs):
            parts, rhs = split_rhs(p)
            i_cat = _cat3(_split_bf16(inv, 2))
            if span < CHUNK:
                both = jnp.dot(jnp.concatenate([_cat3(parts), i_cat], axis=0), rhs, preferred_element_type=F32)
                nxt_p.append(both[:r])
                nxt_inv.append(inv + both[r:])
            else:
                nxt_inv.append(inv + jnp.dot(i_cat, rhs, preferred_element_type=F32))
        ps, invs = nxt_p, nxt_inv
        span *= 2

    def cat6_lhs(x):
        a0, a1, a2 = _split_bf16(x, 3)
        return jnp.concatenate([a0, a0, a0, a1, a1, a2], axis=1)

    def cat6_rhs(x):
        b0, b1, b2 = _split_bf16(x, 3)
        return jnp.concatenate([b0, b1, b2, b0, b1, b0], axis=0)

    resids = [eye - inv - jnp.dot(cat6_lhs(lower), cat6_rhs(inv), preferred_element_type=F32)
              for lower, inv in zip(lowers, invs)]
    return [inv + _dot(inv, resid) for inv, resid in zip(invs, resids)]


def _gdn_kernel(x_ref, gx_ref, w_ref, buf_ref, s0_ref, cw_ref, nal_ref, dt_ref, og_ref,
                o_ref, mq_ref, nbuf_ref, sout_ref,
                xp_ref, act_ref, z_ref, s_ref, gc_ref, kwqw_ref, ku_ref, au_ref, *, heads):
    blk = pl.program_id(1)
    tb = x_ref.shape[0]
    hd = GDN_HEAD_DIM
    width = heads * hd
    conv_ch = 3 * width
    tail = GDN_CONV - 1
    group = min(tb, GDN_GROUP)
    n_sub = group // CHUNK

    @pl.when(blk == 0)
    def _():
        xp_ref[CONV_PAD - tail:CONV_PAD, :] = buf_ref[...]
        s_ref[...] = s0_ref[...]

    xb = _rms(x_ref[...], gx_ref[...]).astype(BF16)

    def project(c0, c1):
        return jnp.dot(xb, w_ref[:, c0:c1], preferred_element_type=F32)

    for c0 in range(0, conv_ch, COL_CHUNK):
        c1 = min(c0 + COL_CHUNK, conv_ch)
        xp_ref[CONV_PAD:CONV_PAD + tb, c0:c1] = project(c0, c1)
        for ct in range(c0 // hd, c1 // hd):
            cs = slice(ct * hd, (ct + 1) * hd)
            y = None
            for j in range(GDN_CONV):
                r0 = CONV_PAD - tail + j
                term = xp_ref[r0:r0 + tb, cs] * cw_ref[j:j + 1, cs]
                y = term if y is None else y + term
            y = y * _sigmoid(y)
            if ct < 2 * heads:
                y = y * lax.rsqrt(jnp.sum(y * y, axis=-1, keepdims=True) + NORM_EPS)
                if ct < heads:
                    y = y * (hd ** -0.5)
            act_ref[:, cs] = y
    for c0 in range(0, width, COL_CHUNK):
        c1 = min(c0 + COL_CHUNK, width)
        z_ref[:, c0:c1] = project(conv_ch + c0, conv_ch + c1)
    ba = project(conv_ch + width, conv_ch + width + LANES)
    mq_ref[...] = project(conv_ch + width + LANES, w_ref.shape[-1])

    new_tail = xp_ref[CONV_PAD + tb - tail:CONV_PAD + tb, :]
    xp_ref[CONV_PAD - tail:CONV_PAD, :] = new_tail

    beta_all = _sigmoid(ba)
    g_all = nal_ref[...] * _softplus(ba + dt_ref[...])

    row = lax.broadcasted_iota(jnp.int32, (group, group), 0)
    col = lax.broadcasted_iota(jnp.int32, (group, group), 1)
    same = (row // CHUNK) == (col // CHUNK)
    causal = jnp.logical_and(same, row >= col)
    strict = jnp.logical_and(same, row > col)
    tri3 = jnp.concatenate([causal.astype(BF16)] * 3, axis=1)
    upper_f = jnp.logical_and(same, row <= col).astype(F32)
    eye = (row == col).astype(F32)
    sub_of_row = lax.broadcasted_iota(jnp.int32, (group, hd), 0) // CHUNK

    pairs = [(gi, h) for gi in range(tb // group) for h in range(heads)]
    gcs = []
    for gi in range(tb // group):
        rs = slice(gi * group, (gi + 1) * group)
        gc = jnp.dot(tri3, jnp.concatenate(_split_bf16(g_all[rs, :], 3), axis=0), preferred_element_type=F32)
        gc_ref[rs, :] = gc
        gcs.append(gc)

    def head_slices(gi, h):
        rs = slice(gi * group, (gi + 1) * group)
        return (act_ref[rs, h * hd:(h + 1) * hd], act_ref[rs, width + h * hd:width + (h + 1) * hd],
                act_ref[rs, 2 * width + h * hd:2 * width + (h + 1) * hd])

    def gate_columns(gi, h):
        gcol = gcs[gi][:, heads + h:heads + h + 1]
        bcol = beta_all[gi * group:(gi + 1) * group, h:h + 1]
        return gcol, bcol

    lowers, a_intras = [], []
    for gi, h in pairs:
        rs = slice(gi * group, (gi + 1) * group)
        gcol, bcol = gate_columns(gi, h)
        grow = jnp.sum(g_all[rs, heads + h:heads + h + 1] * upper_f, axis=0, keepdims=True)
        decay = jnp.where(causal, jnp.exp(jnp.where(causal, gcol - grow, 0.0)), 0.0)
        q, k, _ = head_slices(gi, h)
        kq = _dot_nt(jnp.concatenate([k * bcol, q], axis=0), k)
        lowers.append(kq[:group] * jnp.where(strict, decay, 0.0))
        a_intras.append(kq[group:] * decay)

    invs = _unit_lower_inverses(lowers, eye)

    sols = []
    for (gi, h), inv in zip(pairs, invs):
        gcol, bcol = gate_columns(gi, h)
        _, k, v = head_slices(gi, h)
        sols.append(_dot(inv, jnp.concatenate([v * bcol, k * bcol * jnp.exp(gcol)], axis=-1)).astype(BF16))

    for (gi, h), sol, a_intra in zip(pairs, sols, a_intras):
        gcol, _ = gate_columns(gi, h)
        glast = jnp.concatenate(
            [jnp.broadcast_to(gcol[(s + 1) * CHUNK - 1:(s + 1) * CHUNK, :], (CHUNK, 1)) for s in range(n_sub)],
            axis=0)
        q, k, _ = head_slices(gi, h)
        k_dec = k * jnp.exp(glast - gcol)
        if n_sub > 1:
            k_dec = jnp.concatenate([jnp.where(sub_of_row == s, k_dec, 0.0) for s in range(n_sub)], axis=1)
        kd_sol = _dot_tn(k_dec, sol)
        a_sol = _dot(a_intra, sol)
        qw = q * jnp.exp(gcol) - a_sol[:, hd:]
        for s in range(n_sub):
            idx = (gi * n_sub + s) * heads + h
            ku_ref[idx] = kd_sol[s * hd:(s + 1) * hd, :hd]
            au_ref[idx] = a_sol[s * CHUNK:(s + 1) * CHUNK, :hd]
            kwqw_ref[idx, :hd, :] = kd_sol[s * hd:(s + 1) * hd, hd:].astype(BF16)
            kwqw_ref[idx, hd:, :] = qw[s * CHUNK:(s + 1) * CHUNK, :].astype(BF16)

    for c in range(tb // CHUNK):
        rs = slice(c * CHUNK, (c + 1) * CHUNK)
        for h in range(heads):
            idx = c * heads + h
            gl = heads + h
            s = s_ref[h]
            glast = gc_ref[(c + 1) * CHUNK - 1:(c + 1) * CHUNK, gl:gl + 1]
            r = jnp.dot(kwqw_ref[idx], s.astype(BF16), preferred_element_type=F32)
            s_ref[h] = s * jnp.exp(glast) - r[:hd] + ku_ref[idx]
            o = r[hd:] + au_ref[idx]
            zh = z_ref[rs, h * hd:(h + 1) * hd]
            o_ref[rs, h * hd:(h + 1) * hd] = (_rms(o, og_ref[...]) * (zh * _sigmoid(zh))).astype(o_ref.dtype)

    @pl.when(blk == pl.num_programs(1) - 1)
    def _():
        nbuf_ref[...] = new_tail
        sout_ref[...] = s_ref[...]


def _gdn_mixer(x, gain, w_in, layer, conv_buf, s0, conv_w, neg_exp_alog, dt_bias, o_gain):
    b, t, d_model = x.shape
    heads = s0.shape[1]
    hd = GDN_HEAD_DIM
    width = heads * hd
    cw = 3 * width
    mem_width = w_in.shape[-1] - cw - width - LANES
    tail = GDN_CONV - 1
    tb = min(t, GDN_BLOCK)
    n_ch = (tb // CHUNK) * heads
    tok = lambda n: pl.BlockSpec((None, tb, n), lambda bi, i: (bi, i, 0))
    return pl.pallas_call(
        functools.partial(_gdn_kernel, heads=heads),
        out_shape=[jax.ShapeDtypeStruct((b, t, width), BF16),
                   jax.ShapeDtypeStruct((b, t, mem_width), F32),
                   jax.ShapeDtypeStruct((b, tail, cw), F32),
                   jax.ShapeDtypeStruct((b, heads, hd, hd), F32)],
        grid=(b, t // tb),
        in_specs=[tok(d_model), _const_spec((1, d_model)), _layer_spec(w_in, layer),
                  pl.BlockSpec((None, tail, cw), lambda bi, i: (bi, 0, 0)),
                  pl.BlockSpec((None, heads, hd, hd), lambda bi, i: (bi, 0, 0, 0)),
                  _const_spec((GDN_CONV, cw)), _const_spec((1, LANES)), _const_spec((1, LANES)),
                  _const_spec((1, hd))],
        out_specs=[tok(width), tok(mem_width),
                   pl.BlockSpec((None, tail, cw), lambda bi, i: (bi, 0, 0)),
                   pl.BlockSpec((None, heads, hd, hd), lambda bi, i: (bi, 0, 0, 0))],
        scratch_shapes=[pltpu.VMEM((CONV_PAD + tb, cw), F32), pltpu.VMEM((tb, cw), F32),
                        pltpu.VMEM((tb, width), F32),
                        pltpu.VMEM((heads, hd, hd), F32), pltpu.VMEM((tb, LANES), F32),
                        pltpu.VMEM((n_ch, hd + CHUNK, hd), BF16), pltpu.VMEM((n_ch, hd, hd), F32),
                        pltpu.VMEM((n_ch, CHUNK, hd), F32)],
        compiler_params=_params("parallel", "arbitrary"),
        name="gdn_mixer",
    )(x, gain, w_in, conv_buf, s0, conv_w, neg_exp_alog, dt_bias, o_gain)


def _softplus_logit(z):
    return jnp.log(1.0 + jnp.exp(z))


def _suffix_sums(m, tri2):
    return jnp.dot(jnp.concatenate(_split_bf16(m, 2), axis=1), tri2, preferred_element_type=F32)


def _tri2(n):
    row = lax.broadcasted_iota(jnp.int32, (n, n), 0)
    col = lax.broadcasted_iota(jnp.int32, (n, n), 1)
    tri = (row >= col).astype(BF16)
    return jnp.concatenate([tri, tri], axis=0), row > col


def _sb_attn_kernel(q_ref, kd_ref, vd_ref, kp_ref, vp_ref, kp2_ref, vp2_ref, k_hbm, v_hbm, gq_ref, gk_ref,
                    o_ref, acc_ref, suf_ref, kbuf_ref, vbuf_ref,
                    *, heads, tk, n_off_blocks, interleaved, hbm_block):
    tq = q_ref.shape[0]
    d = SB_HEAD_DIM
    n_off = n_off_blocks(pl.program_id(1))
    cols = [slice(h * d, (h + 1) * d) for h in range(heads)]

    def head_rows(ref, h):
        if interleaved:
            return ref[pl.ds(h, tk, stride=heads), :]
        return ref[:, cols[h]]

    tri_q, strict = _tri2(tq)
    tri_k = tri_q if tk == tq else _tri2(tk)[0]
    zs = [_dot_nt(q_ref[:, cs], kd_ref[:, cs]) for cs in cols]
    cds = [_suffix_sums(jnp.where(strict, _softplus_logit(z), 0.0), tri_q) for z in zs]
    sufs = [c[:, 0:1] for c in cds]
    weights = [jnp.where(strict, jnp.exp(z - c), 0.0) for z, c in zip(zs, cds)]
    valid = n_off > 0
    zps = [_dot_nt(q_ref[:, cs], head_rows(kp_ref, h)) for h, cs in enumerate(cols)]
    cps = [_suffix_sums(_softplus_logit(zp), tri_k) for zp in zps]
    weights_p = [jnp.where(valid, jnp.exp(zp - cp - suf), 0.0) for zp, cp, suf in zip(zps, cps, sufs)]
    sufs = [suf + jnp.where(valid, cp[:, 0:1], 0.0) for suf, cp in zip(sufs, cps)]
    suf_min = None
    for h, cs in enumerate(cols):
        acc_ref[:, cs] = _dot(weights[h], vd_ref[:, cs]) + _dot(weights_p[h], head_rows(vp_ref, h))
        suf_ref[h] = sufs[h]
        hmin = jnp.min(sufs[h])
        suf_min = hmin if suf_min is None else jnp.minimum(suf_min, hmin)

    z_bound = (jnp.max(jnp.abs(gq_ref[...])) * jnp.max(jnp.abs(gk_ref[...]))
               * (math.sqrt(SB_HEAD_DIM) * 1.01))

    def live(carry):
        n, smin = carry
        return jnp.logical_and(n >= 0, z_bound - smin > -F32_MIN_NORMAL_LOG)

    kbuf_ref[...] = kp2_ref[...]
    vbuf_ref[...] = vp2_ref[...]

    def step(carry):
        n, _ = carry

        @pl.when(n < n_off - 2)
        def _():
            pltpu.sync_copy(hbm_block(k_hbm, n), kbuf_ref)
            pltpu.sync_copy(hbm_block(v_hbm, n), vbuf_ref)

        zos = [_dot_nt(q_ref[:, cs], head_rows(kbuf_ref, h)) for h, cs in enumerate(cols)]
        cos = [_suffix_sums(_softplus_logit(zo), tri_k) for zo in zos]
        smin = None
        for h, cs in enumerate(cols):
            suf = suf_ref[h]
            acc_ref[:, cs] += _dot(jnp.exp(zos[h] - cos[h] - suf), head_rows(vbuf_ref, h))
            suf = suf + cos[h][:, 0:1]
            suf_ref[h] = suf
            hmin = jnp.min(suf)
            smin = hmin if smin is None else jnp.minimum(smin, hmin)
        return n - 1, smin

    lax.while_loop(live, step, (n_off - 2, suf_min))
    o_ref[...] = acc_ref[...].astype(o_ref.dtype)


def _sb_attention(q, k_new, v_new, k_off, v_off, gq, gk, tq, tk, causal_prompt, layer=0):
    b, t, width = q.shape
    d = SB_HEAD_DIM
    heads = width // d
    blk = pl.BlockSpec((None, tq, width), lambda bi, i: (bi, i, 0))
    if causal_prompt:
        assert tq == tk
        n_off_blocks = lambda i: i
        prev = pl.BlockSpec((None, tk, width), lambda bi, i: (bi, jnp.maximum(i - 1, 0), 0))
        prev2 = pl.BlockSpec((None, tk, width), lambda bi, i: (bi, jnp.maximum(i - 2, 0), 0))
        hbm_block = lambda ref, n: ref.at[pl.program_id(0), pl.ds(pl.multiple_of(n * tk, tk), tk), :]
        buf = pltpu.VMEM((tk, width), k_off.dtype)
    else:
        t_off = k_off.shape[2] // heads
        assert t_off % tk == 0 and t_off >= tk
        n_blocks = t_off // tk
        n_off_blocks = lambda i: n_blocks
        rows = tk * heads
        prev = pl.BlockSpec((None, None, rows, d), lambda bi, i: (layer, bi, n_blocks - 1, 0))
        prev2 = pl.BlockSpec((None, None, rows, d), lambda bi, i: (layer, bi, max(n_blocks - 2, 0), 0))
        hbm_block = lambda ref, n: ref.at[layer, pl.program_id(0), pl.ds(pl.multiple_of(n * rows, rows), rows), :]
        buf = pltpu.VMEM((rows, d), k_off.dtype)
    hbm = pl.BlockSpec(memory_space=pl.ANY)
    return pl.pallas_call(
        functools.partial(_sb_attn_kernel, heads=heads, tk=tk, n_off_blocks=n_off_blocks,
                          interleaved=not causal_prompt, hbm_block=hbm_block),
        out_shape=jax.ShapeDtypeStruct((b, t, width), BF16),
        grid=(b, t // tq),
        in_specs=[blk, blk, blk, prev, prev, prev2, prev2, hbm, hbm, _const_spec((1, d)), _const_spec((1, d))],
        out_specs=blk,
        scratch_shapes=[pltpu.VMEM((tq, width), F32), pltpu.VMEM((heads, tq, 1), F32), buf, buf],
        compiler_params=_params("parallel", "arbitrary"),
        name="sb_attention",
    )(q, k_new, v_new, k_off, v_off, k_off, v_off, k_off, v_off, gq, gk)


def _head_mean_matrix(width, head_dim):
    idx = jnp.arange(width) // head_dim
    return ((idx[:, None] == idx[None, :]).astype(F32) / head_dim).astype(BF16)


def kernel(x_prompt, x_sample, state_gdn_conv, state_gdn_s, cache_sb_k, cache_sb_v, cache_mem_k, cache_mem_v,
           mem_prompt, norm_mix, norm_mem, norm_ffn, w_in_gdn, w_in_sb, w_mem_kv, mem_q_gain, mem_k_gain,
           gdn_conv_w, gdn_a_log, gdn_dt_bias, gdn_o_gain, sb_q_gain, sb_k_gain, w_out_gdn, w_out_sb, w_up, w_down):
    depth, d_model = norm_mix.shape
    bp, t_p, _ = x_prompt.shape
    bs, t_s, _ = x_sample.shape
    gdn_heads = gdn_a_log.shape[1]
    gdn_width = gdn_heads * GDN_HEAD_DIM
    conv_ch = gdn_conv_w.shape[-1]
    n_sb, _, past, sb_heads, _ = cache_sb_k.shape
    sb_width = sb_heads * SB_HEAD_DIM
    mem_tokens, mem_heads = cache_mem_k.shape[2], cache_mem_k.shape[3]
    mem_width = mem_heads * MEM_HEAD_DIM
    assert MEM_HEAD_DIM & (MEM_HEAD_DIM - 1) == 0

    n_gate = 2 * gdn_heads
    tok_in = conv_ch + gdn_width + n_gate
    w_g = w_in_gdn.astype(BF16)
    w_gdn = jnp.concatenate(
        [w_g[..., :conv_ch + gdn_width],
         jnp.pad(w_g[..., conv_ch + gdn_width:tok_in], ((0, 0), (0, 0), (0, LANES - n_gate))),
         w_g[..., tok_in:]], axis=-1)
    w_sb = w_in_sb.astype(BF16)
    w_kv = w_mem_kv.astype(BF16)
    w_og = w_out_gdn.astype(BF16)
    w_os = w_out_sb.astype(BF16)
    w_u = w_up.astype(BF16)
    w_d = w_down.astype(BF16)

    head_mean = _head_mean_matrix(mem_width, MEM_HEAD_DIM)
    q_gain_t = jnp.tile(mem_q_gain, (1, mem_heads))[:, None, :]
    k_gain_t = jnp.tile(mem_k_gain, (1, mem_heads))[:, None, :]
    lane_pad = lambda a: jnp.pad(a, ((0, 0), (gdn_heads, LANES - n_gate)))[:, None, :]
    neg_exp_alog = lane_pad(-jnp.exp(gdn_a_log))
    dt_bias = lane_pad(gdn_dt_bias)

    pmk, pmv = _mem_kv(mem_prompt, norm_mem[:, None, :], w_kv, k_gain_t, head_mean)
    pmk5 = pmk.reshape(depth, bp, mem_tokens, mem_heads, MEM_HEAD_DIM)
    pmv5 = pmv.reshape(depth, bp, mem_tokens, mem_heads, MEM_HEAD_DIM)
    smk = cache_mem_k.reshape(depth, bs, mem_tokens, mem_width)
    smv = cache_mem_v.reshape(depth, bs, mem_tokens, mem_width)
    cache_k = cache_sb_k.reshape(n_sb, bs, past * sb_heads, SB_HEAD_DIM)
    cache_v = cache_sb_v.reshape(n_sb, bs, past * sb_heads, SB_HEAD_DIM)

    yp, ys = x_prompt, x_sample
    pc, pst, sc, sst = [], [], [], []
    kv_stacks = {True: None, False: None}
    for i in range(depth):
        j = i // 2
        g_mix = norm_mix[i][None, :]
        paths = []
        for y, mk, mv, is_prompt in ((yp, pmk, pmv, True), (ys, smk, smv, False)):
            b, t, _ = y.shape
            y2 = y.reshape(b * t, d_model)
            if i % 2 == 0:
                if is_prompt:
                    buf0 = jnp.zeros((b, GDN_CONV - 1, conv_ch), F32)
                    st0 = jnp.zeros((b, gdn_heads, GDN_HEAD_DIM, GDN_HEAD_DIM), F32)
                else:
                    buf0, st0 = state_gdn_conv[j], state_gdn_s[j]
                o_mix, mq, nbuf, s_new = _gdn_mixer(y, g_mix, w_gdn, j, buf0, st0, gdn_conv_w[j],
                                                    neg_exp_alog[j], dt_bias[j], gdn_o_gain[j][None, :])
                (pc if is_prompt else sc).append(nbuf)
                (pst if is_prompt else sst).append(s_new)
                w_o = w_og
            else:
                gq, gk = sb_q_gain[j][None, :], sb_k_gain[j][None, :]
                q, kb, vb, mq, k_all, v_all = _in_proj_sb(y2, g_mix, w_sb, gq, gk, sb_heads, mem_width,
                                                          j, n_sb, kv_stacks[is_prompt])
                kv_stacks[is_prompt] = (k_all, v_all)
                r3 = lambda a: a.reshape(b, t, sb_width)
                if is_prompt:
                    tq = min(t, SB_QUERY_BLOCK)
                    o_mix = _sb_attention(r3(q), r3(kb), r3(vb), r3(kb), r3(vb), gq, gk, tq, tq, True)
                else:
                    o_mix = _sb_attention(r3(q), r3(kb), r3(vb), cache_k, cache_v, gq, gk, t,
                                          min(past, SB_CACHE_BLOCK), False, j)
                w_o = w_os
            y = _mix_ffn(o_mix, mq.reshape(b, t, mem_width), mk, mv, i, q_gain_t, head_mean, w_o, j, y,
                         norm_ffn[:, None, :], w_u, w_d)
            paths.append(y)
        yp, ys = paths
    heads5 = lambda a, b, t: a.reshape(n_sb, b, t, sb_heads, SB_HEAD_DIM)
    pk, pv = (heads5(a, bp, t_p) for a in kv_stacks[True])
    sk, sv = (heads5(a, bs, t_s) for a in kv_stacks[False])
    return (yp, ys, jnp.stack(pc), jnp.stack(pst), pk, pv, pmk5, pmv5,
            jnp.stack(sc), jnp.stack(sst), sk, sv)
```

```python
import functools
import math

import jax
import jax.numpy as jnp
from jax import lax
from jax.experimental import pallas as pl
from jax.experimental.pallas import tpu as pltpu

F32 = jnp.float32
BF16 = jnp.bfloat16

NORM_EPS = 1e-6
CHUNK = 64
GDN_HEAD_DIM = 128
GDN_CONV = 4
GDN_BLOCK = 256
GDN_GROUP = 128
SB_HEAD_DIM = 128
SB_QUERY_BLOCK = 256
SB_CACHE_BLOCK = 256
MEM_HEAD_DIM = 64
LANES = 128
SUBLANES = 8
VMEM_LIMIT_BYTES = 56 * 1024 * 1024
ROW_TILE = 512
COL_CHUNK = 512
FF_CHUNK = 1024
F32_MIN_NORMAL_LOG = 88.0


def _params(*semantics):
    return pltpu.CompilerParams(dimension_semantics=semantics, vmem_limit_bytes=VMEM_LIMIT_BYTES)


def _dot(a, b):
    return jnp.dot(a.astype(BF16), b.astype(BF16), preferred_element_type=F32)


def _dot_nt(a, b):
    return lax.dot_general(a.astype(BF16), b.astype(BF16), (((1,), (1,)), ((), ())),
                           preferred_element_type=F32)


def _dot_tn(a, b):
    return lax.dot_general(a.astype(BF16), b.astype(BF16), (((0,), (0,)), ((), ())),
                           preferred_element_type=F32)


def _split_bf16(x, terms):
    parts = []
    for _ in range(terms):
        p = x.astype(BF16)
        parts.append(p)
        x = x - p.astype(F32)
    return parts


def _dot_exact_rhs(a, b_exact, terms):
    acc = None
    for p in _split_bf16(a, terms):
        t = jnp.dot(p, b_exact, preferred_element_type=F32)
        acc = t if acc is None else acc + t
    return acc


def _rms(x, gain):
    return x * lax.rsqrt(jnp.mean(x * x, axis=-1, keepdims=True) + NORM_EPS) * gain


def _sigmoid(x):
    return 1.0 / (1.0 + jnp.exp(-x))


def _softplus(x):
    return jnp.maximum(x, 0.0) + jnp.log1p(jnp.exp(-jnp.abs(x)))


def _const_spec(shape):
    zeros = (0,) * len(shape)
    return pl.BlockSpec(shape, lambda *_: zeros)


def _row_spec(tm, n):
    return pl.BlockSpec((tm, n), lambda i: (i, 0))


def _layer_spec(stacked, layer):
    zeros = (0,) * (stacked.ndim - 1)
    return pl.BlockSpec((None,) + stacked.shape[1:], lambda *_: (layer,) + zeros)


def _project(xb, w_ref, o_ref, off, n):
    for c in range(0, n, COL_CHUNK):
        cw = min(COL_CHUNK, n - c)
        o_ref[:, c:c + cw] = jnp.dot(xb, w_ref[:, off + c:off + c + cw],
                                     preferred_element_type=F32).astype(o_ref.dtype)


def _in_proj_sb_kernel(x_ref, g_ref, w_ref, gq_ref, gk_ref, *refs, heads, own, n_fill):
    q_ref, kb_ref, vb_ref, mq_ref, k_ref, v_ref = refs[-6:]
    xb = _rms(x_ref[...], g_ref[...]).astype(BF16)
    tm = x_ref.shape[0]
    d = SB_HEAD_DIM
    width = heads * d
    q_all = jnp.dot(xb, w_ref[:, :width], preferred_element_type=F32)
    k_all = jnp.dot(xb, w_ref[:, width:2 * width], preferred_element_type=F32)
    v_all = jnp.dot(xb, w_ref[:, 2 * width:3 * width], preferred_element_type=F32)
    vb_ref[...] = v_all.astype(BF16)
    for h in range(heads):
        cs = slice(h * d, (h + 1) * d)
        interleaved = pl.ds(h, tm, stride=heads)
        q_ref[:, cs] = (_rms(q_all[:, cs], gq_ref[...]) * (d ** -0.5)).astype(BF16)
        k = _rms(k_all[:, cs], gk_ref[...])
        k_ref[own, interleaved, :] = k
        kb_ref[:, cs] = k.astype(BF16)
        v_ref[own, interleaved, :] = v_all[:, cs]
    for s in range(n_fill):
        if s != own:
            k_ref[s] = jnp.zeros(k_ref.shape[1:], F32)
            v_ref[s] = jnp.zeros(v_ref.shape[1:], F32)
    _project(xb, w_ref, mq_ref, 3 * width, mq_ref.shape[-1])


def _in_proj_sb(x2, gain, w, gq, gk, heads, mem_width, slot, n_slots, kv_all):
    rows, d_model = x2.shape
    d = SB_HEAD_DIM
    width = heads * d
    tm = min(ROW_TILE, rows)
    stack_shape = jax.ShapeDtypeStruct((n_slots, rows * heads, d), F32)
    in_specs = [_row_spec(tm, d_model), _const_spec((1, d_model)), _layer_spec(w, slot),
                _const_spec((1, d)), _const_spec((1, d))]
    args = [x2, gain, w, gq, gk]
    if kv_all is None:
        stack_spec = pl.BlockSpec((n_slots, tm * heads, d), lambda i: (0, i, 0))
        own, n_fill, aliases = slot, n_slots, {}
    else:
        stack_spec = pl.BlockSpec((1, tm * heads, d), lambda i: (slot, i, 0))
        own, n_fill, aliases = 0, 0, {5: 4, 6: 5}
        in_specs += [pl.BlockSpec(memory_space=pl.ANY)] * 2
        args += list(kv_all)
    return pl.pallas_call(
        functools.partial(_in_proj_sb_kernel, heads=heads, own=own, n_fill=n_fill),
        out_shape=[jax.ShapeDtypeStruct((rows, width), BF16)] * 3
        + [jax.ShapeDtypeStruct((rows, mem_width), F32), stack_shape, stack_shape],
        grid=(rows // tm,),
        in_specs=in_specs,
        out_specs=[_row_spec(tm, width)] * 3 + [_row_spec(tm, mem_width), stack_spec, stack_spec],
        input_output_aliases=aliases,
        compiler_params=_params("parallel"),
        name="in_proj_sb",
    )(*args)


def _head_mean_sq(x, hm_ref):
    return _dot_exact_rhs(x * x, hm_ref[...], 2)


def _mem_kv_kernel(mem_ref, g_ref, w_ref, kg_ref, hm_ref, k_ref, v_ref):
    xb = _rms(mem_ref[...], g_ref[...]).astype(BF16)
    width = k_ref.shape[-1]
    k = jnp.dot(xb, w_ref[:, :width], preferred_element_type=F32)
    k_ref[...] = k * lax.rsqrt(_head_mean_sq(k, hm_ref) + NORM_EPS) * kg_ref[...]
    v_ref[...] = jnp.dot(xb, w_ref[:, width:], preferred_element_type=F32)


def _mem_kv(mem, norm_mem, w_kv, k_gain_t, head_mean):
    bp, m, d = mem.shape
    depth = w_kv.shape[0]
    width = w_kv.shape[-1] // 2
    return pl.pallas_call(
        _mem_kv_kernel,
        out_shape=[jax.ShapeDtypeStruct((depth, bp, m, width), F32)] * 2,
        grid=(depth, bp),
        in_specs=[pl.BlockSpec((None, m, d), lambda i, b: (b, 0, 0)),
                  pl.BlockSpec((None, 1, d), lambda i, b: (i, 0, 0)),
                  pl.BlockSpec((None, d, 2 * width), lambda i, b: (i, 0, 0)),
                  pl.BlockSpec((None, 1, width), lambda i, b: (i, 0, 0)),
                  _const_spec((width, width))],
        out_specs=[pl.BlockSpec((None, None, m, width), lambda i, b: (i, b, 0, 0))] * 2,
        compiler_params=_params("parallel", "parallel"),
        name="mem_kv",
    )(mem, norm_mem, w_kv, k_gain_t, head_mean)


def _mix_ffn_kernel(o_ref, mq_ref, k_ref, v_ref, qg_ref, hm_ref, wo_ref, x_ref, gf_ref, wu_ref, wd_ref,
                    y_ref, ymid_ref, mem_ref):
    q = mq_ref[...]
    mw = q.shape[-1]
    qn = q * lax.rsqrt(_head_mean_sq(q, hm_ref) + NORM_EPS) * qg_ref[...]
    lane_head = lax.broadcasted_iota(jnp.int32, (1, mw), 1) // MEM_HEAD_DIM
    n_batch = k_ref.shape[0]
    t = q.shape[0] // n_batch
    heads = mw // MEM_HEAD_DIM
    w_mix = o_ref.shape[-1]
    scores = [[_dot_nt(jnp.where(lane_head == h, qn[bi * t:(bi + 1) * t], 0.0), k_ref[bi].astype(BF16))
               * (MEM_HEAD_DIM ** -0.5) for h in range(heads)] for bi in range(n_batch)]
    ymid_ref[...] = x_ref[...] + _dot(o_ref[...], wo_ref[:w_mix, :])
    for bi in range(n_batch):
        v = v_ref[bi]
        mem = None
        for h, sh in enumerate(scores[bi]):
            e = jnp.exp(sh - jnp.max(sh, axis=-1, keepdims=True))
            p = e / jnp.sum(e, axis=-1, keepdims=True)
            part = _dot(p, jnp.where(lane_head == h, v, 0.0))
            mem = part if mem is None else mem + part
        mem_ref[bi * t:(bi + 1) * t, :] = mem.astype(BF16)
    ymid_ref[...] += jnp.dot(mem_ref[...], wo_ref[w_mix:, :], preferred_element_type=F32)

    hb = _rms(ymid_ref[...], gf_ref[...]).astype(BF16)
    acc = None
    for c in range(0, wu_ref.shape[-1], FF_CHUNK):
        a = jnp.maximum(jnp.dot(hb, wu_ref[:, c:c + FF_CHUNK], preferred_element_type=F32), 0.0)
        t_c = jnp.dot((a * a).astype(BF16), wd_ref[c:c + FF_CHUNK, :], preferred_element_type=F32)
        acc = t_c if acc is None else acc + t_c
    y_ref[...] = ymid_ref[...] + acc


def _resident(spec_fn, *args):
    spec = spec_fn(*args)
    return pl.BlockSpec(spec.block_shape, spec.index_map, pipeline_mode=pl.Buffered(1))


def _mix_ffn(o_mix, mq, mk, mv, layer, q_gain_t, head_mean, w_out, w_layer, x, ffn_gain, w_up, w_down):
    b, t, d = x.shape
    rows = b * t
    w_mix = o_mix.shape[-1]
    mw = mq.shape[-1]
    tm = min(ROW_TILE, rows)
    assert t % tm == 0 or tm % t == 0
    n_batch = max(1, tm // t)
    tiles_per_seq = max(1, t // tm)
    flat = lambda a: a.reshape(rows, a.shape[-1])
    kv = pl.BlockSpec((None, n_batch) + mk.shape[2:], lambda i: (layer, i // tiles_per_seq, 0, 0))
    y = pl.pallas_call(
        _mix_ffn_kernel,
        out_shape=jax.ShapeDtypeStruct((rows, d), F32),
        grid=(rows // tm,),
        in_specs=[_row_spec(tm, w_mix), _row_spec(tm, mw), kv, kv, _layer_spec(q_gain_t, layer),
                  _const_spec((mw, mw)), _resident(_layer_spec, w_out, w_layer), _row_spec(tm, d),
                  _layer_spec(ffn_gain, layer), _resident(_layer_spec, w_up, layer),
                  _resident(_layer_spec, w_down, layer)],
        out_specs=_row_spec(tm, d),
        scratch_shapes=[pltpu.VMEM((tm, d), F32), pltpu.VMEM((tm, mw), BF16)],
        compiler_params=_params("parallel"),
        name="mix_ffn",
    )(flat(o_mix), flat(mq), mk, mv, q_gain_t, head_mean, w_out, flat(x), ffn_gain, w_up, w_down)
    return y.reshape(b, t, d)


CONV_PAD = SUBLANES


def _cat3(parts):
    hi, lo = parts
    return jnp.concatenate([hi, lo, hi], axis=1)


def _unit_lower_inverses(lowers, eye, between=()):
    between = list(between)
    r = eye.shape[0]
    ps = [-lower for lower in lowers]
    invs = [eye + p for p in ps]

    def split_rhs(p):
        parts = _split_bf16(p, 2)
        return parts, jnp.concatenate([parts[0], parts[0], parts[1]], axis=0)

    ps = [jnp.dot(_cat3(parts), rhs, preferred_element_type=F32) for parts, rhs in map(split_rhs, ps)]
    span = 4
    while span <= CHUNK:
        nxt_p, nxt_inv = [], []
        for p, inv in zip(ps, invs):
            parts, rhs = split_rhs(p)
            i_cat = _cat3(_split_bf16(inv, 2))
            if span < CHUNK:
                both = jnp.dot(jnp.concatenate([_cat3(parts), i_cat], axis=0), rhs, preferred_element_type=F32)
                nxt_p.append(both[:r])
                nxt_inv.append(inv + both[r:])
            else:
                nxt_inv.append(inv + jnp.dot(i_cat, rhs, preferred_element_type=F32))
        ps, invs = nxt_p, nxt_inv
        span *= 2
        if between:
            between.pop(0)()

    def cat6_lhs(x):
        a0, a1, a2 = _split_bf16(x, 3)
        return jnp.concatenate([a0, a0, a0, a1, a1, a2], axis=1)

    def cat6_rhs(x):
        b0, b1, b2 = _split_bf16(x, 3)
        return jnp.concatenate([b0, b1, b2, b0, b1, b0], axis=0)

    resids = [eye - inv - jnp.dot(cat6_lhs(lower), cat6_rhs(inv), preferred_element_type=F32)
              for lower, inv in zip(lowers, invs)]
    return [inv + _dot(inv, resid) for inv, resid in zip(invs, resids)]


def _gdn_kernel(x_ref, gx_ref, w_ref, buf_ref, s0_ref, cw_ref, nal_ref, dt_ref, og_ref,
                o_ref, mq_ref, nbuf_ref, sout_ref,
                xp_ref, act_ref, z_ref, s_ref, gc_ref, kwqw_ref, ku_ref, au_ref, *, heads):
    blk = pl.program_id(1)
    tb = x_ref.shape[0]
    hd = GDN_HEAD_DIM
    width = heads * hd
    conv_ch = 3 * width
    tail = GDN_CONV - 1
    group = min(tb, GDN_GROUP)
    n_sub = group // CHUNK

    @pl.when(blk == 0)
    def _():
        xp_ref[CONV_PAD - tail:CONV_PAD, :] = buf_ref[...]
        s_ref[...] = s0_ref[...]

    xb = _rms(x_ref[...], gx_ref[...]).astype(BF16)

    def project(c0, c1):
        return jnp.dot(xb, w_ref[:, c0:c1], preferred_element_type=F32)

    for c0 in range(0, conv_ch, COL_CHUNK):
        c1 = min(c0 + COL_CHUNK, conv_ch)
        xp_ref[CONV_PAD:CONV_PAD + tb, c0:c1] = project(c0, c1)
        for ct in range(c0 // hd, c1 // hd):
            cs = slice(ct * hd, (ct + 1) * hd)
            y = None
            for j in range(GDN_CONV):
                r0 = CONV_PAD - tail + j
                term = xp_ref[r0:r0 + tb, cs] * cw_ref[j:j + 1, cs]
                y = term if y is None else y + term
            y = y * _sigmoid(y)
            if ct < 2 * heads:
                y = y * lax.rsqrt(jnp.sum(y * y, axis=-1, keepdims=True) + NORM_EPS)
                if ct < heads:
                    y = y * (hd ** -0.5)
            act_ref[:, cs] = y
    for c0 in range(0, width, COL_CHUNK):
        c1 = min(c0 + COL_CHUNK, width)
        z_ref[:, c0:c1] = project(conv_ch + c0, conv_ch + c1)
    ba = project(conv_ch + width, conv_ch + width + LANES)
    mq_ref[...] = project(conv_ch + width + LANES, w_ref.shape[-1])

    new_tail = xp_ref[CONV_PAD + tb - tail:CONV_PAD + tb, :]
    xp_ref[CONV_PAD - tail:CONV_PAD, :] = new_tail

    beta_all = _sigmoid(ba)
    g_all = nal_ref[...] * _softplus(ba + dt_ref[...])

    row = lax.broadcasted_iota(jnp.int32, (group, group), 0)
    col = lax.broadcasted_iota(jnp.int32, (group, group), 1)
    same = (row // CHUNK) == (col // CHUNK)
    causal = jnp.logical_and(same, row >= col)
    strict = jnp.logical_and(same, row > col)
    tri3 = jnp.concatenate([causal.astype(BF16)] * 3, axis=1)
    upper_f = jnp.logical_and(same, row <= col).astype(F32)
    eye = (row == col).astype(F32)
    sub_of_row = lax.broadcasted_iota(jnp.int32, (group, hd), 0) // CHUNK

    gcs = []
    for gi in range(tb // group):
        rs = slice(gi * group, (gi + 1) * group)
        gc = jnp.dot(tri3, jnp.concatenate(_split_bf16(g_all[rs, :], 3), axis=0), preferred_element_type=F32)
        gc_ref[rs, :] = gc
        gcs.append(gc)

    def head_slices(gi, h):
        rs = slice(gi * group, (gi + 1) * group)
        return (act_ref[rs, h * hd:(h + 1) * hd], act_ref[rs, width + h * hd:width + (h + 1) * hd],
                act_ref[rs, 2 * width + h * hd:2 * width + (h + 1) * hd])

    def gate_columns(gi, h):
        gcol = gcs[gi][:, heads + h:heads + h + 1]
        bcol = beta_all[gi * group:(gi + 1) * group, h:h + 1]
        return gcol, bcol

    def sweep_chunk(c):
        rs = slice(c * CHUNK, (c + 1) * CHUNK)
        for h in range(heads):
            idx = c * heads + h
            gl = heads + h
            s = s_ref[h]
            glast = gc_ref[(c + 1) * CHUNK - 1:(c + 1) * CHUNK, gl:gl + 1]
            r = jnp.dot(kwqw_ref[idx], s.astype(BF16), preferred_element_type=F32)
            s_ref[h] = s * jnp.exp(glast) - r[:hd] + ku_ref[idx]
            o = r[hd:] + au_ref[idx]
            zh = z_ref[rs, h * hd:(h + 1) * hd]
            o_ref[rs, h * hd:(h + 1) * hd] = (_rms(o, og_ref[...]) * (zh * _sigmoid(zh))).astype(o_ref.dtype)

    ready = []
    for g_idx in range(tb // group):
        pairs = [(g_idx, h) for h in range(heads)]
        lowers, a_intras = [], []
        for gi, h in pairs:
            rs = slice(gi * group, (gi + 1) * group)
            gcol, bcol = gate_columns(gi, h)
            grow = jnp.sum(g_all[rs, heads + h:heads + h + 1] * upper_f, axis=0, keepdims=True)
            decay = jnp.where(causal, jnp.exp(jnp.where(causal, gcol - grow, 0.0)), 0.0)
            q, k, _ = head_slices(gi, h)
            kq = _dot_nt(jnp.concatenate([k * bcol, q], axis=0), k)
            lowers.append(kq[:group] * jnp.where(strict, decay, 0.0))
            a_intras.append(kq[group:] * decay)

        invs = _unit_lower_inverses(lowers, eye, between=[functools.partial(sweep_chunk, c) for c in ready])

        sols = []
        for (gi, h), inv in zip(pairs, invs):
            gcol, bcol = gate_columns(gi, h)
            _, k, v = head_slices(gi, h)
            sols.append(_dot(inv, jnp.concatenate([v * bcol, k * bcol * jnp.exp(gcol)], axis=-1)).astype(BF16))

        for (gi, h), sol, a_intra in zip(pairs, sols, a_intras):
            gcol, _ = gate_columns(gi, h)
            glast = jnp.concatenate(
                [jnp.broadcast_to(gcol[(s + 1) * CHUNK - 1:(s + 1) * CHUNK, :], (CHUNK, 1)) for s in range(n_sub)],
                axis=0)
            q, k, _ = head_slices(gi, h)
            k_dec = k * jnp.exp(glast - gcol)
            if n_sub > 1:
                k_dec = jnp.concatenate([jnp.where(sub_of_row == s, k_dec, 0.0) for s in range(n_sub)], axis=1)
            kd_sol = _dot_tn(k_dec, sol)
            a_sol = _dot(a_intra, sol)
            qw = q * jnp.exp(gcol) - a_sol[:, hd:]
            for s in range(n_sub):
                idx = (gi * n_sub + s) * heads + h
                ku_ref[idx] = kd_sol[s * hd:(s + 1) * hd, :hd]
                au_ref[idx] = a_sol[s * CHUNK:(s + 1) * CHUNK, :hd]
                kwqw_ref[idx, :hd, :] = kd_sol[s * hd:(s + 1) * hd, hd:].astype(BF16)
                kwqw_ref[idx, hd:, :] = qw[s * CHUNK:(s + 1) * CHUNK, :].astype(BF16)

        ready = [g_idx * n_sub + s for s in range(n_sub)]
    for c in ready:
        sweep_chunk(c)

    @pl.when(blk == pl.num_programs(1) - 1)
    def _():
        nbuf_ref[...] = new_tail
        sout_ref[...] = s_ref[...]


def _gdn_mixer(x, gain, w_in, layer, conv_buf, s0, conv_w, neg_exp_alog, dt_bias, o_gain):
    b, t, d_model = x.shape
    heads = s0.shape[1]
    hd = GDN_HEAD_DIM
    width = heads * hd
    cw = 3 * width
    mem_width = w_in.shape[-1] - cw - width - LANES
    tail = GDN_CONV - 1
    tb = min(t, GDN_BLOCK)
    n_ch = (tb // CHUNK) * heads
    tok = lambda n: pl.BlockSpec((None, tb, n), lambda bi, i: (bi, i, 0))
    return pl.pallas_call(
        functools.partial(_gdn_kernel, heads=heads),
        out_shape=[jax.ShapeDtypeStruct((b, t, width), BF16),
                   jax.ShapeDtypeStruct((b, t, mem_width), F32),
                   jax.ShapeDtypeStruct((b, tail, cw), F32),
                   jax.ShapeDtypeStruct((b, heads, hd, hd), F32)],
        grid=(b, t // tb),
        in_specs=[tok(d_model), _const_spec((1, d_model)), _layer_spec(w_in, layer),
                  pl.BlockSpec((None, tail, cw), lambda bi, i: (bi, 0, 0)),
                  pl.BlockSpec((None, heads, hd, hd), lambda bi, i: (bi, 0, 0, 0)),
                  _const_spec((GDN_CONV, cw)), _const_spec((1, LANES)), _const_spec((1, LANES)),
                  _const_spec((1, hd))],
        out_specs=[tok(width), tok(mem_width),
                   pl.BlockSpec((None, tail, cw), lambda bi, i: (bi, 0, 0)),
                   pl.BlockSpec((None, heads, hd, hd), lambda bi, i: (bi, 0, 0, 0))],
        scratch_shapes=[pltpu.VMEM((CONV_PAD + tb, cw), F32), pltpu.VMEM((tb, cw), F32),
                        pltpu.VMEM((tb, width), F32),
                        pltpu.VMEM((heads, hd, hd), F32), pltpu.VMEM((tb, LANES), F32),
                        pltpu.VMEM((n_ch, hd + CHUNK, hd), BF16), pltpu.VMEM((n_ch, hd, hd), F32),
                        pltpu.VMEM((n_ch, CHUNK, hd), F32)],
        compiler_params=_params("parallel", "arbitrary"),
        name="gdn_mixer",
    )(x, gain, w_in, conv_buf, s0, conv_w, neg_exp_alog, dt_bias, o_gain)


def _softplus_logit(z):
    return jnp.log(1.0 + jnp.exp(z))


def _suffix_sums(m, tri2):
    return jnp.dot(jnp.concatenate(_split_bf16(m, 2), axis=1), tri2, preferred_element_type=F32)


def _tri2(n):
    row = lax.broadcasted_iota(jnp.int32, (n, n), 0)
    col = lax.broadcasted_iota(jnp.int32, (n, n), 1)
    tri = (row >= col).astype(BF16)
    return jnp.concatenate([tri, tri], axis=0), row > col


def _sb_attn_kernel(q_ref, kd_ref, vd_ref, kp_ref, vp_ref, kp2_ref, vp2_ref, k_hbm, v_hbm, gq_ref, gk_ref,
                    o_ref, acc_ref, suf_ref, kbuf_ref, vbuf_ref,
                    *, heads, tk, n_off_blocks, interleaved, hbm_block):
    tq = q_ref.shape[0]
    d = SB_HEAD_DIM
    n_off = n_off_blocks(pl.program_id(1))
    cols = [slice(h * d, (h + 1) * d) for h in range(heads)]

    def head_rows(ref, h):
        if interleaved:
            return ref[pl.ds(h, tk, stride=heads), :]
        return ref[:, cols[h]]

    tri_q, strict = _tri2(tq)
    tri_k = tri_q if tk == tq else _tri2(tk)[0]
    zs = [_dot_nt(q_ref[:, cs], kd_ref[:, cs]) for cs in cols]
    cds = [_suffix_sums(jnp.where(strict, _softplus_logit(z), 0.0), tri_q) for z in zs]
    sufs = [c[:, 0:1] for c in cds]
    weights = [jnp.where(strict, jnp.exp(z - c), 0.0) for z, c in zip(zs, cds)]
    valid = n_off > 0
    zps = [_dot_nt(q_ref[:, cs], head_rows(kp_ref, h)) for h, cs in enumerate(cols)]
    cps = [_suffix_sums(_softplus_logit(zp), tri_k) for zp in zps]
    weights_p = [jnp.where(valid, jnp.exp(zp - cp - suf), 0.0) for zp, cp, suf in zip(zps, cps, sufs)]
    sufs = [suf + jnp.where(valid, cp[:, 0:1], 0.0) for suf, cp in zip(sufs, cps)]
    suf_min = None
    for h, cs in enumerate(cols):
        acc_ref[:, cs] = _dot(weights[h], vd_ref[:, cs]) + _dot(weights_p[h], head_rows(vp_ref, h))
        suf_ref[h] = sufs[h]
        hmin = jnp.min(sufs[h])
        suf_min = hmin if suf_min is None else jnp.minimum(suf_min, hmin)

    z_bound = (jnp.max(jnp.abs(gq_ref[...])) * jnp.max(jnp.abs(gk_ref[...]))
               * (math.sqrt(SB_HEAD_DIM) * 1.01))

    def live(carry):
        n, smin = carry
        return jnp.logical_and(n >= 0, z_bound - smin > -F32_MIN_NORMAL_LOG)

    kbuf_ref[...] = kp2_ref[...]
    vbuf_ref[...] = vp2_ref[...]

    def step(carry):
        n, _ = carry

        @pl.when(n < n_off - 2)
        def _():
            pltpu.sync_copy(hbm_block(k_hbm, n), kbuf_ref)
            pltpu.sync_copy(hbm_block(v_hbm, n), vbuf_ref)

        zos = [_dot_nt(q_ref[:, cs], head_rows(kbuf_ref, h)) for h, cs in enumerate(cols)]
        cos = [_suffix_sums(_softplus_logit(zo), tri_k) for zo in zos]
        smin = None
        for h, cs in enumerate(cols):
            suf = suf_ref[h]
            acc_ref[:, cs] += _dot(jnp.exp(zos[h] - cos[h] - suf), head_rows(vbuf_ref, h))
            suf = suf + cos[h][:, 0:1]
            suf_ref[h] = suf
            hmin = jnp.min(suf)
            smin = hmin if smin is None else jnp.minimum(smin, hmin)
        return n - 1, smin

    lax.while_loop(live, step, (n_off - 2, suf_min))
    o_ref[...] = acc_ref[...].astype(o_ref.dtype)


def _sb_attention(q, k_new, v_new, k_off, v_off, gq, gk, tq, tk, causal_prompt, layer=0):
    b, t, width = q.shape
    d = SB_HEAD_DIM
    heads = width // d
    blk = pl.BlockSpec((None, tq, width), lambda bi, i: (bi, i, 0))
    if causal_prompt:
        assert tq == tk
        n_off_blocks = lambda i: i
        prev = pl.BlockSpec((None, tk, width), lambda bi, i: (bi, jnp.maximum(i - 1, 0), 0))
        prev2 = pl.BlockSpec((None, tk, width), lambda bi, i: (bi, jnp.maximum(i - 2, 0), 0))
        hbm_block = lambda ref, n: ref.at[pl.program_id(0), pl.ds(pl.multiple_of(n * tk, tk), tk), :]
        buf = pltpu.VMEM((tk, width), k_off.dtype)
    else:
        t_off = k_off.shape[2] // heads
        assert t_off % tk == 0 and t_off >= tk
        n_blocks = t_off // tk
        n_off_blocks = lambda i: n_blocks
        rows = tk * heads
        prev = pl.BlockSpec((None, None, rows, d), lambda bi, i: (layer, bi, n_blocks - 1, 0))
        prev2 = pl.BlockSpec((None, None, rows, d), lambda bi, i: (layer, bi, max(n_blocks - 2, 0), 0))
        hbm_block = lambda ref, n: ref.at[layer, pl.program_id(0), pl.ds(pl.multiple_of(n * rows, rows), rows), :]
        buf = pltpu.VMEM((rows, d), k_off.dtype)
    hbm = pl.BlockSpec(memory_space=pl.ANY)
    return pl.pallas_call(
        functools.partial(_sb_attn_kernel, heads=heads, tk=tk, n_off_blocks=n_off_blocks,
                          interleaved=not causal_prompt, hbm_block=hbm_block),
        out_shape=jax.ShapeDtypeStruct((b, t, width), BF16),
        grid=(b, t // tq),
        in_specs=[blk, blk, blk, prev, prev, prev2, prev2, hbm, hbm, _const_spec((1, d)), _const_spec((1, d))],
        out_specs=blk,
        scratch_shapes=[pltpu.VMEM((tq, width), F32), pltpu.VMEM((heads, tq, 1), F32), buf, buf],
        compiler_params=_params("parallel", "arbitrary"),
        name="sb_attention",
    )(q, k_new, v_new, k_off, v_off, k_off, v_off, k_off, v_off, gq, gk)


def _head_mean_matrix(width, head_dim):
    idx = jnp.arange(width) // head_dim
    return ((idx[:, None] == idx[None, :]).astype(F32) / head_dim).astype(BF16)


def kernel(x_prompt, x_sample, state_gdn_conv, state_gdn_s, cache_sb_k, cache_sb_v, cache_mem_k, cache_mem_v,
           mem_prompt, norm_mix, norm_mem, norm_ffn, w_in_gdn, w_in_sb, w_mem_kv, mem_q_gain, mem_k_gain,
           gdn_conv_w, gdn_a_log, gdn_dt_bias, gdn_o_gain, sb_q_gain, sb_k_gain, w_out_gdn, w_out_sb, w_up, w_down):
    depth, d_model = norm_mix.shape
    bp, t_p, _ = x_prompt.shape
    bs, t_s, _ = x_sample.shape
    gdn_heads = gdn_a_log.shape[1]
    gdn_width = gdn_heads * GDN_HEAD_DIM
    conv_ch = gdn_conv_w.shape[-1]
    n_sb, _, past, sb_heads, _ = cache_sb_k.shape
    sb_width = sb_heads * SB_HEAD_DIM
    mem_tokens, mem_heads = cache_mem_k.shape[2], cache_mem_k.shape[3]
    mem_width = mem_heads * MEM_HEAD_DIM
    assert MEM_HEAD_DIM & (MEM_HEAD_DIM - 1) == 0

    n_gate = 2 * gdn_heads
    tok_in = conv_ch + gdn_width + n_gate
    w_g = w_in_gdn.astype(BF16)
    w_gdn = jnp.concatenate(
        [w_g[..., :conv_ch + gdn_width],
         jnp.pad(w_g[..., conv_ch + gdn_width:tok_in], ((0, 0), (0, 0), (0, LANES - n_gate))),
         w_g[..., tok_in:]], axis=-1)
    w_sb = w_in_sb.astype(BF16)
    w_kv = w_mem_kv.astype(BF16)
    w_og = w_out_gdn.astype(BF16)
    w_os = w_out_sb.astype(BF16)
    w_u = w_up.astype(BF16)
    w_d = w_down.astype(BF16)

    head_mean = _head_mean_matrix(mem_width, MEM_HEAD_DIM)
    q_gain_t = jnp.tile(mem_q_gain, (1, mem_heads))[:, None, :]
    k_gain_t = jnp.tile(mem_k_gain, (1, mem_heads))[:, None, :]
    lane_pad = lambda a: jnp.pad(a, ((0, 0), (gdn_heads, LANES - n_gate)))[:, None, :]
    neg_exp_alog = lane_pad(-jnp.exp(gdn_a_log))
    dt_bias = lane_pad(gdn_dt_bias)

    pmk, pmv = _mem_kv(mem_prompt, norm_mem[:, None, :], w_kv, k_gain_t, head_mean)
    pmk5 = pmk.reshape(depth, bp, mem_tokens, mem_heads, MEM_HEAD_DIM)
    pmv5 = pmv.reshape(depth, bp, mem_tokens, mem_heads, MEM_HEAD_DIM)
    smk = cache_mem_k.reshape(depth, bs, mem_tokens, mem_width)
    smv = cache_mem_v.reshape(depth, bs, mem_tokens, mem_width)
    cache_k = cache_sb_k.reshape(n_sb, bs, past * sb_heads, SB_HEAD_DIM)
    cache_v = cache_sb_v.reshape(n_sb, bs, past * sb_heads, SB_HEAD_DIM)

    yp, ys = x_prompt, x_sample
    pc, pst, sc, sst = [], [], [], []
    kv_stacks = {True: None, False: None}
    for i in range(depth):
        j = i // 2
        g_mix = norm_mix[i][None, :]
        paths = []
        for y, mk, mv, is_prompt in ((yp, pmk, pmv, True), (ys, smk, smv, False)):
            b, t, _ = y.shape
            y2 = y.reshape(b * t, d_model)
            if i % 2 == 0:
                if is_prompt:
                    buf0 = jnp.zeros((b, GDN_CONV - 1, conv_ch), F32)
                    st0 = jnp.zeros((b, gdn_heads, GDN_HEAD_DIM, GDN_HEAD_DIM), F32)
                else:
                    buf0, st0 = state_gdn_conv[j], state_gdn_s[j]
                o_mix, mq, nbuf, s_new = _gdn_mixer(y, g_mix, w_gdn, j, buf0, st0, gdn_conv_w[j],
                                                    neg_exp_alog[j], dt_bias[j], gdn_o_gain[j][None, :])
                (pc if is_prompt else sc).append(nbuf)
                (pst if is_prompt else sst).append(s_new)
                w_o = w_og
            else:
                gq, gk = sb_q_gain[j][None, :], sb_k_gain[j][None, :]
                q, kb, vb, mq, k_all, v_all = _in_proj_sb(y2, g_mix, w_sb, gq, gk, sb_heads, mem_width,
                                                          j, n_sb, kv_stacks[is_prompt])
                kv_stacks[is_prompt] = (k_all, v_all)
                r3 = lambda a: a.reshape(b, t, sb_width)
                if is_prompt:
                    tq = min(t, SB_QUERY_BLOCK)
                    o_mix = _sb_attention(r3(q), r3(kb), r3(vb), r3(kb), r3(vb), gq, gk, tq, tq, True)
                else:
                    o_mix = _sb_attention(r3(q), r3(kb), r3(vb), cache_k, cache_v, gq, gk, t,
                                          min(past, SB_CACHE_BLOCK), False, j)
                w_o = w_os
            y = _mix_ffn(o_mix, mq.reshape(b, t, mem_width), mk, mv, i, q_gain_t, head_mean, w_o, j, y,
                         norm_ffn[:, None, :], w_u, w_d)
            paths.append(y)
        yp, ys = paths
    heads5 = lambda a, b, t: a.reshape(n_sb, b, t, sb_heads, SB_HEAD_DIM)
    pk, pv = (heads5(a, bp, t_p) for a in kv_stacks[True])
    sk, sv = (heads5(a, bs, t_s) for a in kv_stacks[False])
    return (yp, ys, jnp.stack(pc), jnp.stack(pst), pk, pv, pmk5, pmv5,
            jnp.stack(sc), jnp.stack(sst), sk, sv)
```
